```python
import math
import jax, jax.numpy as jnp
from jax import lax
import numpy as np

D_MODEL = 1024
BATCH = 16
SEQ = 256
DEPTH = 1
DEC_BATCH = 8
DEC_SEQ = 4096
PAST_LEN = 256

GRID_W = 64
N_HEADS = 8
QK_HALF = 64
QK_DIM = 2 * QK_HALF
V_DIM = 2 * QK_HALF
ATTN_WIDTH = N_HEADS * V_DIM
ROPE_BASE = 10000.0
Q_BLOCK = 128
POOL_WINDOWS = (2, 4, 8, 16)
POOL_GROUPS = len(POOL_WINDOWS)
POOL_GROUP_DIM = 128
POOL_WIDTH = POOL_GROUPS * POOL_GROUP_DIM
N_EXPERTS = 32
TOP_K = 4
D_EXPERT = 512
SWIGLU_ALPHA = 1.702
SWIGLU_LIMIT = 7.0
RMS_EPS = 1e-6
IN_WIDTH = POOL_WIDTH + 2 * N_HEADS * QK_DIM + ATTN_WIDTH + 2 * D_MODEL
IN_SPLITS = (POOL_WIDTH,
             POOL_WIDTH + N_HEADS * QK_DIM,
             POOL_WIDTH + 2 * N_HEADS * QK_DIM,
             POOL_WIDTH + 2 * N_HEADS * QK_DIM + ATTN_WIDTH,
             POOL_WIDTH + 2 * N_HEADS * QK_DIM + ATTN_WIDTH + D_MODEL)

kernel_name = "hybrid_pool_diffattn_moe_diffusion_step"


def rmsnorm(x, g):
    x32 = x.astype(jnp.float32)
    y = x32 * lax.rsqrt(jnp.mean(x32 * x32, axis=-1, keepdims=True) + RMS_EPS)
    return (y * g.astype(jnp.float32)).astype(x.dtype)


def adaln(c, w_ada, b_ada):
    mod = jax.nn.silu(c) @ w_ada + b_ada
    mod = mod.reshape(c.shape[0], 1, 6, D_MODEL)
    return (mod[:, :, 0], mod[:, :, 1], mod[:, :, 2], mod[:, :, 3], mod[:, :, 4], mod[:, :, 5])


def axial_rope(x):
    n = x.shape[1]
    rows = n // GRID_W
    row = jnp.repeat(jnp.arange(rows), GRID_W).astype(jnp.float32)
    col = jnp.tile(jnp.arange(GRID_W), rows).astype(jnp.float32)
    n_freq = QK_HALF // 4
    inv = ROPE_BASE ** (-jnp.arange(n_freq, dtype=jnp.float32) / n_freq)
    ang = jnp.concatenate([row[:, None] * inv, col[:, None] * inv], axis=-1)
    cos = jnp.cos(ang)[None, :, None, None, :]
    sin = jnp.sin(ang)[None, :, None, None, :]
    x32 = x.astype(jnp.float32)
    x1, x2 = x32[..., :QK_HALF // 2], x32[..., QK_HALF // 2:]
    return jnp.concatenate([x1 * cos - x2 * sin, x1 * sin + x2 * cos], axis=-1).astype(x.dtype)


def diff_lambda(lp, lam_init):
    lp = lp.astype(jnp.float32)
    return jnp.exp(jnp.sum(lp[0] * lp[1])) - jnp.exp(jnp.sum(lp[2] * lp[3])) + lam_init


def diff_attention(q, k, v, lam):
    b, lq = q.shape[0], q.shape[1]
    n_blocks = lq // Q_BLOCK
    qb = q.reshape(b, n_blocks, Q_BLOCK, N_HEADS, 2, QK_HALF).transpose(1, 0, 2, 3, 4, 5)
    scale = QK_HALF ** -0.5

    def one_block(qi):
        s = jnp.einsum('bqhcd,bkhcd->bchqk', qi, k).astype(jnp.float32) * scale
        p = jax.nn.softmax(s, axis=-1)
        a = p[:, 0] - lam * p[:, 1]
        return jnp.einsum('bhqk,bkhd->bqhd', a.astype(v.dtype), v)

    o = lax.map(one_block, qb)
    return o.transpose(1, 0, 2, 3, 4).reshape(b, lq, N_HEADS, V_DIM)


def multiscale_pool(u, w_mix, scale):
    b, n, _ = u.shape
    u32 = u.astype(jnp.float32)
    cs = jnp.pad(jnp.cumsum(u32, axis=1), ((0, 0), (1, 0), (0, 0)))
    t = jnp.arange(n)
    groups = []
    for g, w in enumerate(POOL_WINDOWS):
        lo = jnp.clip(t - w // 2, 0, n)
        hi = jnp.clip(t + w - w // 2, 0, n)
        sl = slice(g * POOL_GROUP_DIM, (g + 1) * POOL_GROUP_DIM)
        window_sum = cs[:, hi, sl] - cs[:, lo, sl]
        count = (hi - lo).astype(jnp.float32)[None, :, None]
        groups.append(window_sum / count - u32[:, :, sl])
    pooled = jnp.stack(groups, axis=2)
    mixed = jnp.einsum('blgc,gcd->blgd', pooled, w_mix.astype(jnp.float32))
    return (mixed.reshape(b, n, POOL_WIDTH) * scale.astype(jnp.float32)).astype(u.dtype)


def token_mixer(h, p, lam_init, ctx_k, ctx_v):
    b, n, _ = h.shape
    proj = h @ p['w_in']
    u, q, k, v, g_pool, g_attn = jnp.split(proj, IN_SPLITS, axis=-1)
    q = q.reshape(b, n, N_HEADS, 2, QK_HALF)
    k = k.reshape(b, n, N_HEADS, 2, QK_HALF)
    v = v.reshape(b, n, N_HEADS, V_DIM)
    if ctx_k is None:
        keys, vals = k, v
    else:
        q = axial_rope(q)
        keys = jnp.concatenate(
            [ctx_k.reshape(b, ctx_k.shape[1], N_HEADS, 2, QK_HALF).astype(k.dtype), axial_rope(k)], axis=1)
        vals = jnp.concatenate([ctx_v.astype(v.dtype), v], axis=1)
    lam = diff_lambda(p['lambda_params'], lam_init)
    attn = diff_attention(q, keys, vals, lam)
    attn = rmsnorm(attn, p['subln_g']) * (1.0 - lam_init)
    attn = attn.reshape(b, n, ATTN_WIDTH)
    pool = multiscale_pool(u, p['w_pool_mix'], p['pool_scale'])
    merged = (jax.nn.sigmoid(g_pool) * (pool @ p['w_pool_br'])
              + jax.nn.sigmoid(g_attn) * (attn @ p['w_attn_br']))
    return merged @ p['w_out'], k.reshape(b, n, N_HEADS, QK_DIM), v


def moe_ffn(h, p):
    b, n, d = h.shape
    t = h.reshape(b * n, d)
    logits = (t @ p['w_router'] + p['b_router']).astype(jnp.float32)
    top_val, top_idx = lax.top_k(logits, TOP_K)
    top_w = jax.nn.softmax(top_val, axis=-1)
    gates = jnp.einsum('tk,tke->te', top_w, jax.nn.one_hot(top_idx, N_EXPERTS, dtype=jnp.float32))

    def expert_step(acc, xs):
        w_e_in, b_e_in, w_e_out, b_e_out, gate_e = xs
        gu = t @ w_e_in + b_e_in
        g_lin = jnp.minimum(gu[:, :D_EXPERT], SWIGLU_LIMIT)
        u_lin = jnp.clip(gu[:, D_EXPERT:], -SWIGLU_LIMIT, SWIGLU_LIMIT)
        hid = (u_lin + 1.0) * (g_lin * jax.nn.sigmoid(SWIGLU_ALPHA * g_lin))
        y = hid @ w_e_out + b_e_out
        return acc + gate_e[:, None] * y.astype(jnp.float32), None

    out, _ = lax.scan(expert_step, jnp.zeros((b * n, d), jnp.float32),
                      (p['w_moe_in'], p['b_moe_in'], p['w_moe_out'], p['b_moe_out'], gates.T))
    return out.astype(h.dtype).reshape(b, n, d)


def trunk_layer(x, mod, p, lam_init, ctx_k, ctx_v):
    shift_a, scale_a, gate_a, shift_f, scale_f, gate_f = mod
    g = p['norm_gains']
    h = rmsnorm(x, g[0]) * (1.0 + scale_a) + shift_a
    mixed, k, v = token_mixer(h, p, lam_init, ctx_k, ctx_v)
    x = x + gate_a * rmsnorm(mixed, g[1])
    h = rmsnorm(x, g[2]) * (1.0 + scale_f) + shift_f
    x = x + gate_f * rmsnorm(moe_ffn(h, p), g[3])
    return x, k, v


def setup_inputs(seed: int = 0) -> dict:
    key = jax.random.key(seed)
    ks = jax.random.split(key, 23)

    def nrm(k, shape, scale):
        return jax.random.normal(k, shape, jnp.float32) * scale

    return {
        'x_prompt': nrm(ks[0], (BATCH, SEQ, D_MODEL), 1.0),
        'x_sample': nrm(ks[1], (DEC_BATCH, DEC_SEQ, D_MODEL), 1.0),
        'c': nrm(ks[2], (DEC_BATCH, D_MODEL), 1.0),
        'cache_k': nrm(ks[3], (DEC_BATCH, DEPTH, PAST_LEN, N_HEADS, QK_DIM), 1.0),
        'cache_v': nrm(ks[4], (DEC_BATCH, DEPTH, PAST_LEN, N_HEADS, V_DIM), 1.0),
        'c_ctx': nrm(ks[5], (D_MODEL,), 1.0),
        'w_ada': nrm(ks[6], (DEPTH, D_MODEL, 6 * D_MODEL), 0.5 * D_MODEL ** -0.5),
        'b_ada': nrm(ks[7], (DEPTH, 6 * D_MODEL), 0.02),
        'norm_gains': 1.0 + nrm(ks[8], (DEPTH, 4, D_MODEL), 0.05),
        'w_in': nrm(ks[9], (DEPTH, D_MODEL, IN_WIDTH), D_MODEL ** -0.5),
        'w_pool_mix': nrm(ks[10], (DEPTH, POOL_GROUPS, POOL_GROUP_DIM, POOL_GROUP_DIM), POOL_GROUP_DIM ** -0.5),
        'pool_scale': 1.0 + nrm(ks[11], (DEPTH, POOL_WIDTH), 0.05),
        'lambda_params': nrm(ks[12], (DEPTH, 4, QK_HALF), 0.1),
        'subln_g': 1.0 + nrm(ks[13], (DEPTH, V_DIM), 0.05),
        'w_pool_br': nrm(ks[14], (DEPTH, POOL_WIDTH, D_MODEL), POOL_WIDTH ** -0.5),
        'w_attn_br': nrm(ks[15], (DEPTH, ATTN_WIDTH, D_MODEL), ATTN_WIDTH ** -0.5),
        'w_out': nrm(ks[16], (DEPTH, D_MODEL, D_MODEL), D_MODEL ** -0.5),
        'w_router': nrm(ks[17], (DEPTH, D_MODEL, N_EXPERTS), D_MODEL ** -0.5),
        'b_router': nrm(ks[18], (DEPTH, N_EXPERTS), 0.01),
        'w_moe_in': nrm(ks[19], (DEPTH, N_EXPERTS, D_MODEL, 2 * D_EXPERT), D_MODEL ** -0.5),
        'b_moe_in': nrm(ks[20], (DEPTH, N_EXPERTS, 2 * D_EXPERT), 0.02),
        'w_moe_out': nrm(ks[21], (DEPTH, N_EXPERTS, D_EXPERT, D_MODEL), D_EXPERT ** -0.5),
        'b_moe_out': nrm(ks[22], (DEPTH, N_EXPERTS, D_MODEL), 0.02),
    }


def reference(x_prompt, x_sample, c, cache_k, cache_v, c_ctx, w_ada, b_ada, norm_gains, w_in,
              w_pool_mix, pool_scale, lambda_params, subln_g, w_pool_br, w_attn_br, w_out,
              w_router, b_router, w_moe_in, b_moe_in, w_moe_out, b_moe_out):
    y_prompt, y_sample = x_prompt, x_sample
    ctx_keys, ctx_vals = [], []
    for l in range(DEPTH):
        lam_init = 0.8 - 0.6 * math.exp(-0.3 * l)
        p = {
            'norm_gains': norm_gains[l], 'w_in': w_in[l], 'w_pool_mix': w_pool_mix[l],
            'pool_scale': pool_scale[l], 'lambda_params': lambda_params[l], 'subln_g': subln_g[l],
            'w_pool_br': w_pool_br[l], 'w_attn_br': w_attn_br[l], 'w_out': w_out[l],
            'w_router': w_router[l], 'b_router': b_router[l], 'w_moe_in': w_moe_in[l],
            'b_moe_in': b_moe_in[l], 'w_moe_out': w_moe_out[l], 'b_moe_out': b_moe_out[l],
        }
        mod_ctx = adaln(c_ctx[None, :], w_ada[l], b_ada[l])
        mod_lat = adaln(c, w_ada[l], b_ada[l])
        y_prompt, k_ctx, v_ctx = trunk_layer(y_prompt, mod_ctx, p, lam_init, None, None)
        ctx_keys.append(k_ctx)
        ctx_vals.append(v_ctx)
        y_sample, _, _ = trunk_layer(y_sample, mod_lat, p, lam_init, cache_k[:, l], cache_v[:, l])
    new_k = jnp.stack(ctx_keys, axis=1)
    new_v = jnp.stack(ctx_vals, axis=1)
    return (y_prompt, y_sample, new_k, new_v)
```

```python
import functools
import math

import jax
import jax.numpy as jnp
from jax import lax
from jax.experimental import pallas as pl
from jax.experimental.pallas import tpu as pltpu

D_MODEL = 1024
GRID_W = 64
N_HEADS = 8
QK_HALF = 64
QK_DIM = 2 * QK_HALF
V_DIM = 128
ROPE_BASE = 10000.0
POOL_WINDOWS = (2, 4, 8, 16)
POOL_GROUP_DIM = 128
POOL_WIDTH = len(POOL_WINDOWS) * POOL_GROUP_DIM
POOL_HALO = max(POOL_WINDOWS) // 2
N_EXPERTS = 32
TOP_K = 4
D_EXPERT = 512
SWIGLU_ALPHA = 1.702
SWIGLU_LIMIT = 7.0
RMS_EPS = 1e-6
LAM_INIT = 0.8 - 0.6 * math.exp(-0.3 * 0)

V7X_LANES = 128
V7X_VMEM_LIMIT_BYTES = 56 * 1024 * 1024

COL_CHUNK = 512
KEY_CHUNK = 256
TAIL_BLOCK = 256
EXPERT_TILE = 512
ROW_BLOCK = 256
DMA_UNROLL = 32

BF16 = jnp.bfloat16
F32 = jnp.float32


def _params(*semantics):
    return pltpu.CompilerParams(dimension_semantics=semantics,
                                vmem_limit_bytes=V7X_VMEM_LIMIT_BYTES)


def _split_bf16(x):
    hi = x.astype(BF16)
    lo = (x - hi.astype(F32)).astype(BF16)
    return hi, lo


def _dot3(x, w):
    xh, xl = _split_bf16(x)
    wh, wl = _split_bf16(w)
    dot = functools.partial(jnp.dot, preferred_element_type=F32)
    return dot(xh, wh) + (dot(xl, wh) + dot(xh, wl))


def _rms(x, gain):
    return x * lax.rsqrt(jnp.mean(x * x, axis=-1, keepdims=True) + RMS_EPS) * gain


def _ada_kernel(c_ref, w_ref, b_ref, o_ref):
    c = c_ref[...]
    o_ref[...] = _dot3(c * jax.nn.sigmoid(c), w_ref[...]) + b_ref[...]


def _modulation(cc, w_ada, b_ada):
    rows, d = cc.shape
    n = w_ada.shape[1]
    return pl.pallas_call(
        _ada_kernel,
        out_shape=jax.ShapeDtypeStruct((rows, n), F32),
        grid=(n // COL_CHUNK,),
        in_specs=[pl.BlockSpec((rows, d), lambda j: (0, 0)),
                  pl.BlockSpec((d, COL_CHUNK), lambda j: (0, j)),
                  pl.BlockSpec((1, COL_CHUNK), lambda j: (0, j))],
        out_specs=pl.BlockSpec((rows, COL_CHUNK), lambda j: (0, j)),
        compiler_params=_params("arbitrary"),
        name="modulation",
    )(cc, w_ada, b_ada)


def _rope_slab(x, cos, sin_signed, first_half):
    partner = jnp.where(first_half, pltpu.roll(x, V7X_LANES - QK_HALF // 2, 1),
                        pltpu.roll(x, QK_HALF // 2, 1))
    return x * cos + partner * sin_signed


def _inproj_kernel(x_ref, mod_ref, g_ref, w_ref, *rest, rope, keep_kv):
    if rope:
        cos_ref, sin_ref, *outs = rest
    else:
        outs = list(rest)
    if keep_kv:
        u_ref, q_ref, k_ref, v_ref, gp_ref, ga_ref, nk_ref, nv_ref = outs
    else:
        u_ref, q_ref, k_ref, v_ref, gp_ref, ga_ref = outs
    d = D_MODEL
    x = x_ref[0]
    mod = mod_ref[0]
    h = _rms(x, g_ref[0:1, :]) * (1.0 + mod[:, d:2 * d]) + mod[:, 0:d]
    hb = h.astype(BF16)
    if rope:
        cos = cos_ref[...]
        sin = sin_ref[...]
        lane = lax.broadcasted_iota(jnp.int32, cos.shape, 1)
        first_half = (lane % QK_HALF) < (QK_HALF // 2)

    def proj(col):
        return jnp.dot(hb, w_ref[:, col:col + COL_CHUNK], preferred_element_type=F32)

    def rotate(r):
        if not rope:
            return r
        slabs = [_rope_slab(r[:, s:s + V7X_LANES], cos, sin, first_half)
                 for s in range(0, COL_CHUNK, V7X_LANES)]
        return jnp.concatenate(slabs, axis=1)

    u_ref[0] = proj(0)
    q0 = POOL_WIDTH
    k0 = q0 + d
    v0 = k0 + d
    gp0 = v0 + d
    ga0 = gp0 + d
    for c in range(0, d, COL_CHUNK):
        q_ref[0, :, c:c + COL_CHUNK] = (rotate(proj(q0 + c)) * (QK_HALF ** -0.5)).astype(BF16)
        kc = proj(k0 + c)
        k_ref[0, :, c:c + COL_CHUNK] = rotate(kc).astype(BF16)
        vc = proj(v0 + c)
        v_ref[0, :, c:c + COL_CHUNK] = vc.astype(BF16)
        if keep_kv:
            nk_ref[0, :, c:c + COL_CHUNK] = kc
            nv_ref[0, :, c:c + COL_CHUNK] = vc
        gp_ref[0, :, c:c + COL_CHUNK] = proj(gp0 + c).astype(BF16)
        ga_ref[0, :, c:c + COL_CHUNK] = proj(ga0 + c).astype(BF16)


def _in_projection(x, mod, mod_row, gains, w_in, rope_tables, tm, keep_kv):
    b, n, d = x.shape
    rope = rope_tables is not None
    row_spec = lambda w: pl.BlockSpec((1, tm, w), lambda i, j: (i, j, 0))
    in_specs = [row_spec(d),
                pl.BlockSpec((1, 1, mod.shape[-1]), lambda i, j: (mod_row(i), 0, 0)),
                pl.BlockSpec(gains.shape, lambda i, j: (0, 0)),
                pl.BlockSpec(w_in.shape, lambda i, j: (0, 0), pipeline_mode=pl.Buffered(1))]
    args = [x, mod, gains, w_in]
    if rope:
        in_specs += [pl.BlockSpec((tm, V7X_LANES), lambda i, j: (j, 0))] * 2
        args += list(rope_tables)
    widths = [(POOL_WIDTH, F32)] + [(d, BF16)] * 5 + ([(d, F32)] * 2 if keep_kv else [])
    return pl.pallas_call(
        functools.partial(_inproj_kernel, rope=rope, keep_kv=keep_kv),
        out_shape=[jax.ShapeDtypeStruct((b, n, w), t) for w, t in widths],
        grid=(b, n // tm),
        in_specs=in_specs,
        out_specs=[row_spec(w) for w, _ in widths],
        compiler_params=_params("parallel", "parallel"),
        name="in_projection_rope" if rope else "in_projection",
    )(*args)


def _rope_tables(n):
    rows = n // GRID_W
    row = jnp.repeat(jnp.arange(rows), GRID_W).astype(F32)
    col = jnp.tile(jnp.arange(GRID_W), rows).astype(F32)
    n_freq = QK_HALF // 4
    inv = ROPE_BASE ** (-jnp.arange(n_freq, dtype=F32) / n_freq)
    ang = jnp.concatenate([row[:, None] * inv, col[:, None] * inv], axis=-1)
    cos = jnp.cos(ang)
    sin = jnp.sin(ang)
    unit_cos = jnp.concatenate([cos, cos], axis=-1)
    unit_sin = jnp.concatenate([-sin, sin], axis=-1)
    reps = V7X_LANES // QK_HALF
    return jnp.tile(unit_cos, (1, reps)), jnp.tile(unit_sin, (1, reps))


def _attn_kernel(lp_ref, sg_ref, q_ref, *rest, tq, n_cache_chunks, n_self_chunks):
    if n_cache_chunks:
        ck_ref, cv_ref, k_ref, v_ref, o_ref, s_ref = rest
    else:
        k_ref, v_ref, o_ref, s_ref = rest
    lp = lp_ref[...]
    lam = (jnp.exp(jnp.sum(lp[0:1] * lp[1:2], axis=-1, keepdims=True))
           - jnp.exp(jnp.sum(lp[2:3] * lp[3:4], axis=-1, keepdims=True)) + LAM_INIT)

    q = q_ref[0]
    lane = lax.broadcasted_iota(jnp.int32, q.shape, 1)
    zero = jnp.zeros_like(q)
    qq = jnp.concatenate([jnp.where(lane < QK_HALF, q, zero),
                          jnp.where(lane >= QK_HALF, q, zero)], axis=0)

    def keys(c):
        if c < n_cache_chunks:
            return ck_ref[0, c * KEY_CHUNK:(c + 1) * KEY_CHUNK, :]
        c -= n_cache_chunks
        return k_ref[0, c * KEY_CHUNK:(c + 1) * KEY_CHUNK, :]

    def values(c):
        if c < n_cache_chunks:
            return cv_ref[0, c * KEY_CHUNK:(c + 1) * KEY_CHUNK, :]
        c -= n_cache_chunks
        return v_ref[0, c * KEY_CHUNK:(c + 1) * KEY_CHUNK, :]

    n_chunks = n_cache_chunks + n_self_chunks
    m = jnp.full((2 * tq, KEY_CHUNK), -jnp.inf, F32)
    for c in range(n_chunks):
        s = lax.dot_general(qq, keys(c), (((1,), (1,)), ((), ())), preferred_element_type=F32)
        s_ref[c] = s
        m = jnp.maximum(m, s)
    m = jnp.max(m, axis=-1, keepdims=True)

    l = jnp.zeros((2 * tq, KEY_CHUNK), F32)
    for c in range(n_chunks):
        p = jnp.exp(s_ref[c] - m)
        s_ref[c] = p
        l = l + p
    l = jnp.sum(l, axis=-1, keepdims=True)
    l0 = l[:tq]
    l1 = l[tq:]
    ratio = lam * l0 / l1

    acc = jnp.zeros((tq, V_DIM), F32)
    for c in range(n_chunks):
        a = s_ref[c, 0:tq, :] - ratio * s_ref[c, tq:2 * tq, :]
        acc = acc + jnp.dot(a.astype(BF16), values(c), preferred_element_type=F32)
    o = acc / l0
    o_ref[0] = (_rms(o, sg_ref[...]) * (1.0 - LAM_INIT)).astype(BF16)


def _diff_attention(lambda_params, subln_g, q, k, v, cache_k, cache_v, tq):
    b, n, _ = q.shape
    n_self_chunks = n // KEY_CHUNK
    n_cache_chunks = 0 if cache_k is None else cache_k.shape[1] // KEY_CHUNK
    head_q = pl.BlockSpec((1, tq, QK_DIM), lambda i, h, j: (i, j, h))
    in_specs = [pl.BlockSpec(lambda_params.shape, lambda i, h, j: (0, 0)),
                pl.BlockSpec(subln_g.shape, lambda i, h, j: (0, 0)),
                head_q]
    args = [lambda_params, subln_g, q]
    if n_cache_chunks:
        in_specs += [pl.BlockSpec((1, cache_k.shape[1], QK_DIM), lambda i, h, j: (i, 0, h))] * 2
        args += [cache_k, cache_v]
    in_specs += [pl.BlockSpec((1, n, QK_DIM), lambda i, h, j: (i, 0, h))] * 2
    args += [k, v]
    n_chunks = n_cache_chunks + n_self_chunks
    return pl.pallas_call(
        functools.partial(_attn_kernel, tq=tq, n_cache_chunks=n_cache_chunks,
                          n_self_chunks=n_self_chunks),
        out_shape=jax.ShapeDtypeStruct((b, n, N_HEADS * V_DIM), BF16),
        grid=(b, N_HEADS, n // tq),
        in_specs=in_specs,
        out_specs=head_q,
        scratch_shapes=[pltpu.VMEM((n_chunks, 2 * tq, KEY_CHUNK), F32)],
        compiler_params=_params("parallel", "parallel", "arbitrary"),
        name="diff_attention_cached" if n_cache_chunks else "diff_attention",
    )(*args)


def _post_kernel(x_ref, u_ref, at_ref, gp_ref, ga_ref, mod_ref, g_ref, wmix_ref, ps_ref,
                 wpb_ref, wab_ref, wo_ref, wr_ref, br_ref,
                 x1_ref, h2_ref, idx_ref, wt_ref, upad_ref, *, tm, n):
    d = D_MODEL
    j = pl.program_id(1)
    r0 = pl.multiple_of(j * tm, tm)
    halo = POOL_HALO

    upad_ref[halo:halo + tm, :] = u_ref[0, pl.ds(r0, tm), :]
    upad_ref[0:halo, :] = jnp.zeros((halo, POOL_WIDTH), F32)
    upad_ref[halo + tm:, :] = jnp.zeros((halo, POOL_WIDTH), F32)

    @pl.when(j > 0)
    def _():
        upad_ref[0:halo, :] = u_ref[0, pl.ds(pl.multiple_of(r0 - halo, halo), halo), :]

    @pl.when(j < pl.num_programs(1) - 1)
    def _():
        upad_ref[halo + tm:, :] = u_ref[0, pl.ds(pl.multiple_of(r0 + tm, halo), halo), :]

    t = r0 + lax.broadcasted_iota(jnp.int32, (tm, POOL_GROUP_DIM), 0)
    mixed = []
    for g, w in enumerate(POOL_WINDOWS):
        sl = slice(g * POOL_GROUP_DIM, (g + 1) * POOL_GROUP_DIM)
        window = upad_ref[halo - w // 2:halo - w // 2 + tm, sl]
        for off in range(-w // 2 + 1, w // 2):
            window = window + upad_ref[halo + off:halo + off + tm, sl]
        count = (jnp.minimum(t + w // 2, n) - jnp.maximum(t - w // 2, 0)).astype(F32)
        pooled = window / count - upad_ref[halo:halo + tm, sl]
        mixed.append(jnp.dot(pooled.astype(BF16), wmix_ref[g], preferred_element_type=F32))
    pool = (jnp.concatenate(mixed, axis=1) * ps_ref[...]).astype(BF16)

    pool_b = jnp.dot(pool, wpb_ref[...], preferred_element_type=F32)
    attn_b = jnp.dot(at_ref[0], wab_ref[...], preferred_element_type=F32)
    merged = (jax.nn.sigmoid(gp_ref[0].astype(F32)) * pool_b
              + jax.nn.sigmoid(ga_ref[0].astype(F32)) * attn_b)
    mix = jnp.dot(merged.astype(BF16), wo_ref[...], preferred_element_type=F32)

    mod = mod_ref[0]
    x1 = x_ref[0] + mod[:, 2 * d:3 * d] * _rms(mix, g_ref[1:2, :])
    x1_ref[0] = x1
    h2 = _rms(x1, g_ref[2:3, :]) * (1.0 + mod[:, 4 * d:5 * d]) + mod[:, 3 * d:4 * d]
    h2_ref[0] = h2

    logits = _dot3(h2, wr_ref[...]) + br_ref[...]
    lane = lax.broadcasted_iota(jnp.int32, logits.shape, 1).astype(F32)
    vals, idxs = [], []
    for _ in range(TOP_K):
        mx = jnp.max(logits, axis=-1, keepdims=True)
        first = jnp.min(jnp.where(logits == mx, lane, float(N_EXPERTS)), axis=-1, keepdims=True)
        vals.append(mx)
        idxs.append(first)
        logits = jnp.where(lane == first, -jnp.inf, logits)
    e = [jnp.exp(v - vals[0]) for v in vals]
    denom = e[0] + e[1] + e[2] + e[3]
    idx_ref[0] = jnp.concatenate(idxs, axis=1).astype(jnp.int32)
    wt_ref[0] = jnp.concatenate([ek / denom for ek in e], axis=1)


def _mixer_tail(x, u, attn, g_pool, g_attn, mod, mod_row, gains, w_pool_mix, pool_scale,
                w_pool_br, w_attn_br, w_out, w_router, b_router, tm):
    b, n, d = x.shape
    row_spec = lambda w: pl.BlockSpec((1, tm, w), lambda i, j: (i, j, 0))
    whole = lambda a: pl.BlockSpec(a.shape, lambda i, j: (0,) * a.ndim,
                                   pipeline_mode=pl.Buffered(1))
    in_specs = [row_spec(d),
                pl.BlockSpec((1, n, POOL_WIDTH), lambda i, j: (i, 0, 0)),
                row_spec(d), row_spec(d), row_spec(d),
                pl.BlockSpec((1, 1, mod.shape[-1]), lambda i, j: (mod_row(i), 0, 0)),
                whole(gains), whole(w_pool_mix), whole(pool_scale), whole(w_pool_br),
                whole(w_attn_br), whole(w_out), whole(w_router), whole(b_router)]
    return pl.pallas_call(
        functools.partial(_post_kernel, tm=tm, n=n),
        out_shape=[jax.ShapeDtypeStruct((b, n, d), F32),
                   jax.ShapeDtypeStruct((b, n, d), F32),
                   jax.ShapeDtypeStruct((b, n, TOP_K), jnp.int32),
                   jax.ShapeDtypeStruct((b, n, TOP_K), F32)],
        grid=(b, n // tm),
        in_specs=in_specs,
        out_specs=[row_spec(d), row_spec(d), row_spec(TOP_K), row_spec(TOP_K)],
        scratch_shapes=[pltpu.VMEM((tm + 2 * POOL_HALO, POOL_WIDTH), F32)],
        compiler_params=_params("parallel", "arbitrary"),
        name="mixer_tail",
    )(x, u, attn, g_pool, g_attn, mod, gains, w_pool_mix, pool_scale, w_pool_br, w_attn_br,
      w_out, w_router, b_router)


def _row_copies(n_rows, make_copy):
    def issue(blk, carry):
        for r in range(DMA_UNROLL):
            for k in range(TOP_K):
                make_copy(blk * DMA_UNROLL + r, k).start()
        return carry

    def drain(blk, carry):
        for r in range(DMA_UNROLL):
            for k in range(TOP_K):
                make_copy(blk * DMA_UNROLL + r, k).wait()
        return carry

    lax.fori_loop(0, n_rows // DMA_UNROLL, issue, 0)
    lax.fori_loop(0, n_rows // DMA_UNROLL, drain, 0)


def _dispatch_kernel(pos_ref, h_ref, xs_in_ref, xs_ref, sem):
    del xs_in_ref

    def make_copy(r, k):
        return pltpu.make_async_copy(h_ref.at[pl.ds(r, 1)],
                                     xs_ref.at[pl.ds(pos_ref[0, 0, r * TOP_K + k], 1)], sem)

    _row_copies(ROW_BLOCK, make_copy)


def _dispatch(pos, h2, xs):
    t, d = h2.shape
    n_blk = t // ROW_BLOCK
    return pl.pallas_call(
        _dispatch_kernel,
        out_shape=jax.ShapeDtypeStruct(xs.shape, xs.dtype),
        grid=(n_blk,),
        in_specs=[pl.BlockSpec((1, 1, ROW_BLOCK * TOP_K), lambda i: (i, 0, 0),
                               memory_space=pltpu.SMEM),
                  pl.BlockSpec((ROW_BLOCK, d), lambda i: (i, 0)),
                  pl.BlockSpec(memory_space=pl.ANY)],
        out_specs=pl.BlockSpec(memory_space=pl.ANY),
        scratch_shapes=[pltpu.SemaphoreType.DMA(())],
        input_output_aliases={2: 0},
        compiler_params=_params("arbitrary"),
        name="expert_dispatch",
    )(pos.reshape(n_blk, 1, ROW_BLOCK * TOP_K), h2, xs)


def _moe_kernel(te_ref, nt_ref, x_ref, wi_ref, bi_ref, wo_ref, bo_ref, y_ref):
    i = pl.program_id(0)

    @pl.when(i < nt_ref[0])
    def _():
        gu = jnp.dot(x_ref[...].astype(BF16), wi_ref[0], preferred_element_type=F32) + bi_ref[0]
        g_lin = jnp.minimum(gu[:, :D_EXPERT], SWIGLU_LIMIT)
        u_lin = jnp.clip(gu[:, D_EXPERT:], -SWIGLU_LIMIT, SWIGLU_LIMIT)
        hid = (u_lin + 1.0) * (g_lin * jax.nn.sigmoid(SWIGLU_ALPHA * g_lin))
        y_ref[...] = jnp.dot(hid.astype(BF16), wo_ref[0], preferred_element_type=F32) + bo_ref[0]

    @pl.when(i >= nt_ref[0])
    def _():
        y_ref[...] = jnp.zeros_like(y_ref)


def _grouped_mlp(tile_expert, n_tiles, xs, w_in, b_in, w_out, b_out):
    p, d = xs.shape
    max_tiles = p // EXPERT_TILE

    def active(i, te, nt):
        return jnp.minimum(i, nt[0] - 1)

    def expert(i, te, nt):
        return te[active(i, te, nt)]

    grid_spec = pltpu.PrefetchScalarGridSpec(
        num_scalar_prefetch=2,
        grid=(max_tiles,),
        in_specs=[pl.BlockSpec((EXPERT_TILE, d), lambda i, te, nt: (active(i, te, nt), 0)),
                  pl.BlockSpec((1, d, 2 * D_EXPERT), lambda i, te, nt: (expert(i, te, nt), 0, 0)),
                  pl.BlockSpec((1, 1, 2 * D_EXPERT), lambda i, te, nt: (expert(i, te, nt), 0, 0)),
                  pl.BlockSpec((1, D_EXPERT, d), lambda i, te, nt: (expert(i, te, nt), 0, 0)),
                  pl.BlockSpec((1, 1, d), lambda i, te, nt: (expert(i, te, nt), 0, 0))],
        out_specs=pl.BlockSpec((EXPERT_TILE, d), lambda i, te, nt: (i, 0)))
    return pl.pallas_call(
        _moe_kernel,
        out_shape=jax.ShapeDtypeStruct((p, d), F32),
        grid_spec=grid_spec,
        compiler_params=_params("arbitrary"),
        name="grouped_expert_mlp",
    )(tile_expert, n_tiles, xs, w_in, b_in, w_out, b_out)


def _combine_kernel(pos_ref, wt_ref, x1_ref, mod_ref, g_ref, ys_ref, o_ref, buf_ref, sem):
    d = D_MODEL

    def make_copy(r, k):
        return pltpu.make_async_copy(ys_ref.at[pl.ds(pos_ref[0, 0, r * TOP_K + k], 1)],
                                     buf_ref.at[k, pl.ds(r, 1)], sem)

    _row_copies(ROW_BLOCK, make_copy)
    wt = wt_ref[...]
    moe = wt[:, 0:1] * buf_ref[0]
    for k in range(1, TOP_K):
        moe = moe + wt[:, k:k + 1] * buf_ref[k]
    mod = mod_ref[0]
    o_ref[...] = x1_ref[...] + mod[:, 5 * d:6 * d] * _rms(moe, g_ref[3:4, :])


def _combine(pos, wt, x1, mod, mod_row, gains, ys):
    t, d = x1.shape
    n_blk = t // ROW_BLOCK
    return pl.pallas_call(
        _combine_kernel,
        out_shape=jax.ShapeDtypeStruct((t, d), F32),
        grid=(n_blk,),
        in_specs=[pl.BlockSpec((1, 1, ROW_BLOCK * TOP_K), lambda i: (i, 0, 0),
                               memory_space=pltpu.SMEM),
                  pl.BlockSpec((ROW_BLOCK, TOP_K), lambda i: (i, 0)),
                  pl.BlockSpec((ROW_BLOCK, d), lambda i: (i, 0)),
                  pl.BlockSpec((1, 1, mod.shape[-1]), lambda i: (mod_row(i), 0, 0)),
                  pl.BlockSpec(gains.shape, lambda i: (0, 0)),
                  pl.BlockSpec(memory_space=pl.ANY)],
        out_specs=pl.BlockSpec((ROW_BLOCK, d), lambda i: (i, 0)),
        scratch_shapes=[pltpu.VMEM((TOP_K, ROW_BLOCK, d), F32), pltpu.SemaphoreType.DMA(())],
        compiler_params=_params("arbitrary"),
        name="expert_combine",
    )(pos.reshape(n_blk, 1, ROW_BLOCK * TOP_K), wt, x1, mod, gains, ys)


def _routing_tables(idx):
    t = idx.shape[0]
    onehot = idx[:, :, None] == jnp.arange(N_EXPERTS, dtype=jnp.int32)
    mask = jnp.any(onehot, axis=1).astype(jnp.int32)
    csum = jnp.cumsum(mask, axis=0)
    rank = csum - mask
    counts = csum[-1]
    tiles = (counts + EXPERT_TILE - 1) // EXPERT_TILE
    tile_end = jnp.cumsum(tiles)
    row_start = (tile_end - tiles) * EXPERT_TILE
    dest = row_start[None, :] + rank
    pos = jnp.sum(jnp.where(onehot, dest[:, None, :], 0), axis=-1)
    max_tiles = (t * TOP_K) // EXPERT_TILE + N_EXPERTS
    tile_expert = jnp.sum(jnp.arange(max_tiles, dtype=jnp.int32)[:, None] >= tile_end[None, :],
                          axis=-1)
    tile_expert = jnp.minimum(tile_expert, N_EXPERTS - 1).astype(jnp.int32)
    return pos.astype(jnp.int32), tile_expert, tile_end[-1:].astype(jnp.int32), max_tiles


def kernel(x_prompt, x_sample, c, cache_k, cache_v, c_ctx, w_ada, b_ada, norm_gains, w_in,
           w_pool_mix, pool_scale, lambda_params, subln_g, w_pool_br, w_attn_br, w_out,
           w_router, b_router, w_moe_in, b_moe_in, w_moe_out, b_moe_out):
    depth = w_ada.shape[0]
    assert depth == 1, "one trunk layer"
    d = D_MODEL
    nb, ns, _ = x_prompt.shape
    db, dn, _ = x_sample.shape
    past = cache_k.shape[2]
    l = 0

    ctx_row = db
    mod_rows = -(-(db + 1) // 8) * 8
    cc = jnp.zeros((mod_rows, d), F32).at[:db].set(c).at[ctx_row].set(c_ctx)
    mod = _modulation(cc, w_ada[l], b_ada[l][None, :])[:, None, :]

    gains = norm_gains[l]
    w_in_b = w_in[l].astype(BF16)
    w_mix_b = w_pool_mix[l].astype(BF16)
    w_pb_b = w_pool_br[l].astype(BF16)
    w_ab_b = w_attn_br[l].astype(BF16)
    w_o_b = w_out[l].astype(BF16)
    ps = pool_scale[l][None, :]
    sg = subln_g[l][None, :]
    lp = lambda_params[l]
    wr = w_router[l]
    br = b_router[l][None, :]
    ck = cache_k[:, l].reshape(db, past, N_HEADS * QK_DIM).astype(BF16)
    cv = cache_v[:, l].reshape(db, past, N_HEADS * V_DIM).astype(BF16)

    ctx_mod = lambda i: ctx_row
    lat_mod = lambda i: i

    def mixer(x, mod_row, rope_tables, cache, tm, tq, keep_kv):
        outs = _in_projection(x, mod, mod_row, gains, w_in_b, rope_tables, tm, keep_kv)
        u, q, k, v, gp, ga = outs[:6]
        attn = _diff_attention(lp, sg, q, k, v, cache[0], cache[1], tq)
        x1, h2, idx, wt = _mixer_tail(x, u, attn, gp, ga, mod, mod_row, gains, w_mix_b, ps,
                                      w_pb_b, w_ab_b, w_o_b, wr, br, min(tm, TAIL_BLOCK))
        return x1, h2, idx, wt, outs[6:]

    x1_c, h2_c, idx_c, wt_c, (new_k, new_v) = mixer(x_prompt, ctx_mod, None, (None, None),
                                                    min(ns, 256), min(ns, 256), True)
    x1_s, h2_s, idx_s, wt_s, _ = mixer(x_sample, lat_mod, _rope_tables(dn), (ck, cv),
                                       512, 512, False)

    t_c = nb * ns
    t_s = db * dn
    idx_all = jnp.concatenate([idx_c.reshape(t_c, TOP_K), idx_s.reshape(t_s, TOP_K)], axis=0)
    pos, tile_expert, n_tiles, max_tiles = _routing_tables(idx_all)

    xs = jnp.zeros((max_tiles * EXPERT_TILE, d), F32)
    xs = _dispatch(pos[:t_c], h2_c.reshape(t_c, d), xs)
    xs = _dispatch(pos[t_c:], h2_s.reshape(t_s, d), xs)
    ys = _grouped_mlp(tile_expert, n_tiles, xs, w_moe_in[l].astype(BF16), b_moe_in[l][:, None, :],
                      w_moe_out[l].astype(BF16), b_moe_out[l][:, None, :])

    y_c = _combine(pos[:t_c], wt_c.reshape(t_c, TOP_K), x1_c.reshape(t_c, d), mod,
                   ctx_mod, gains, ys)
    blocks_per_request = dn // ROW_BLOCK
    y_s = _combine(pos[t_c:], wt_s.reshape(t_s, TOP_K), x1_s.reshape(t_s, d), mod,
                   lambda i: i // blocks_per_request, gains, ys)

    return (y_c.reshape(nb, ns, d), y_s.reshape(db, dn, d),
            new_k.reshape(nb, 1, ns, N_HEADS, QK_DIM), new_v.reshape(nb, 1, ns, N_HEADS, V_DIM))
```

```python
import functools
import math

import jax
import jax.numpy as jnp
from jax import lax
from jax.experimental import pallas as pl
from jax.experimental.pallas import tpu as pltpu

D_MODEL = 1024
GRID_W = 64
N_HEADS = 8
QK_HALF = 64
QK_DIM = 2 * QK_HALF
V_DIM = 128
ROPE_BASE = 10000.0
POOL_WINDOWS = (2, 4, 8, 16)
POOL_GROUP_DIM = 128
POOL_WIDTH = len(POOL_WINDOWS) * POOL_GROUP_DIM
POOL_HALO = max(POOL_WINDOWS) // 2
N_EXPERTS = 32
TOP_K = 4
D_EXPERT = 512
SWIGLU_ALPHA = 1.702
SWIGLU_LIMIT = 7.0
RMS_EPS = 1e-6
LAM_INIT = 0.8 - 0.6 * math.exp(-0.3 * 0)

V7X_LANES = 128
V7X_VMEM_LIMIT_BYTES = 56 * 1024 * 1024

COL_CHUNK = 512
KEY_CHUNK = 256
Q_SCALE = QK_HALF ** -0.5 * math.log2(math.e)
TAIL_BLOCK = 256
EXPERT_TILE = 512
ROW_BLOCK = 256
DMA_UNROLL = 32

BF16 = jnp.bfloat16
F32 = jnp.float32


def _params(*semantics):
    return pltpu.CompilerParams(dimension_semantics=semantics,
                                vmem_limit_bytes=V7X_VMEM_LIMIT_BYTES)


def _split_bf16(x):
    hi = x.astype(BF16)
    lo = (x - hi.astype(F32)).astype(BF16)
    return hi, lo


def _dot3(x, w):
    xh, xl = _split_bf16(x)
    wh, wl = _split_bf16(w)
    dot = functools.partial(jnp.dot, preferred_element_type=F32)
    return dot(xh, wh) + (dot(xl, wh) + dot(xh, wl))


def _rms(x, gain):
    return x * lax.rsqrt(jnp.mean(x * x, axis=-1, keepdims=True) + RMS_EPS) * gain


def _ada_kernel(c_ref, w_ref, b_ref, o_ref):
    c = c_ref[...]
    o_ref[...] = _dot3(c * jax.nn.sigmoid(c), w_ref[...]) + b_ref[...]


def _modulation(cc, w_ada, b_ada):
    rows, d = cc.shape
    n = w_ada.shape[1]
    return pl.pallas_call(
        _ada_kernel,
        out_shape=jax.ShapeDtypeStruct((rows, n), F32),
        grid=(n // COL_CHUNK,),
        in_specs=[pl.BlockSpec((rows, d), lambda j: (0, 0)),
                  pl.BlockSpec((d, COL_CHUNK), lambda j: (0, j)),
                  pl.BlockSpec((1, COL_CHUNK), lambda j: (0, j))],
        out_specs=pl.BlockSpec((rows, COL_CHUNK), lambda j: (0, j)),
        compiler_params=_params("arbitrary"),
        name="modulation",
    )(cc, w_ada, b_ada)


def _rope_slab(x, cos, sin_signed, first_half):
    partner = jnp.where(first_half, pltpu.roll(x, V7X_LANES - QK_HALF // 2, 1),
                        pltpu.roll(x, QK_HALF // 2, 1))
    return x * cos + partner * sin_signed


def _inproj_kernel(x_ref, mod_ref, g_ref, w_ref, *rest, rope, keep_kv):
    if rope:
        cos_ref, sin_ref, *outs = rest
    else:
        outs = list(rest)
    if keep_kv:
        u_ref, qt_ref, k_ref, vt_ref, gp_ref, ga_ref, nk_ref, nv_ref = outs
    else:
        u_ref, qt_ref, k_ref, vt_ref, gp_ref, ga_ref = outs
    d = D_MODEL
    x = x_ref[0]
    mod = mod_ref[0]
    h = _rms(x, g_ref[0:1, :]) * (1.0 + mod[:, d:2 * d]) + mod[:, 0:d]
    hb = h.astype(BF16)
    if rope:
        cos = cos_ref[...]
        sin = sin_ref[...]
        lane = lax.broadcasted_iota(jnp.int32, cos.shape, 1)
        first_half = (lane % QK_HALF) < (QK_HALF // 2)

    def proj(col):
        return jnp.dot(hb, w_ref[:, col:col + COL_CHUNK], preferred_element_type=F32)

    def rotate(r):
        if not rope:
            return r
        slabs = [_rope_slab(r[:, s:s + V7X_LANES], cos, sin, first_half)
                 for s in range(0, COL_CHUNK, V7X_LANES)]
        return jnp.concatenate(slabs, axis=1)

    u_ref[0] = proj(0)
    q0 = POOL_WIDTH
    k0 = q0 + d
    v0 = k0 + d
    gp0 = v0 + d
    ga0 = gp0 + d
    for c in range(0, d, COL_CHUNK):
        qt_ref[0, c:c + COL_CHUNK, :] = (rotate(proj(q0 + c)) * Q_SCALE).T.astype(BF16)
        kc = proj(k0 + c)
        k_ref[0, :, c:c + COL_CHUNK] = rotate(kc).astype(BF16)
        vc = proj(v0 + c)
        vt_ref[0, c:c + COL_CHUNK, :] = vc.T.astype(BF16)
        if keep_kv:
            nk_ref[0, :, c:c + COL_CHUNK] = kc
            nv_ref[0, :, c:c + COL_CHUNK] = vc
        gp_ref[0, :, c:c + COL_CHUNK] = proj(gp0 + c).astype(BF16)
        ga_ref[0, :, c:c + COL_CHUNK] = proj(ga0 + c).astype(BF16)


def _in_projection(x, mod, mod_row, gains, w_in, rope_tables, tm, keep_kv):
    b, n, d = x.shape
    rope = rope_tables is not None
    row_spec = lambda w: pl.BlockSpec((1, tm, w), lambda i, j: (i, j, 0))
    in_specs = [row_spec(d),
                pl.BlockSpec((1, 1, mod.shape[-1]), lambda i, j: (mod_row(i), 0, 0)),
                pl.BlockSpec(gains.shape, lambda i, j: (0, 0)),
                pl.BlockSpec(w_in.shape, lambda i, j: (0, 0), pipeline_mode=pl.Buffered(1))]
    args = [x, mod, gains, w_in]
    if rope:
        in_specs += [pl.BlockSpec((tm, V7X_LANES), lambda i, j: (j, 0))] * 2
        args += list(rope_tables)
    token_major = lambda w, t: (jax.ShapeDtypeStruct((b, n, w), t), row_spec(w))
    feature_major = (jax.ShapeDtypeStruct((b, d, n), BF16),
                     pl.BlockSpec((1, d, tm), lambda i, j: (i, 0, j)))
    outs = [token_major(POOL_WIDTH, F32), feature_major, token_major(d, BF16), feature_major,
            token_major(d, BF16), token_major(d, BF16)]
    if keep_kv:
        outs += [token_major(d, F32)] * 2
    return pl.pallas_call(
        functools.partial(_inproj_kernel, rope=rope, keep_kv=keep_kv),
        out_shape=[shape for shape, _ in outs],
        grid=(b, n // tm),
        in_specs=in_specs,
        out_specs=[spec for _, spec in outs],
        compiler_params=_params("arbitrary", "arbitrary"),
        name="in_projection_rope" if rope else "in_projection",
    )(*args)


def _rope_tables(n):
    rows = n // GRID_W
    row = jnp.repeat(jnp.arange(rows), GRID_W).astype(F32)
    col = jnp.tile(jnp.arange(GRID_W), rows).astype(F32)
    n_freq = QK_HALF // 4
    inv = ROPE_BASE ** (-jnp.arange(n_freq, dtype=F32) / n_freq)
    ang = jnp.concatenate([row[:, None] * inv, col[:, None] * inv], axis=-1)
    cos = jnp.cos(ang)
    sin = jnp.sin(ang)
    unit_cos = jnp.concatenate([cos, cos], axis=-1)
    unit_sin = jnp.concatenate([-sin, sin], axis=-1)
    reps = V7X_LANES // QK_HALF
    return jnp.tile(unit_cos, (1, reps)), jnp.tile(unit_sin, (1, reps))


def _attn_kernel(lp_ref, sg_ref, q_ref, *rest, tq, n_cache_chunks, n_self_chunks):
    if n_cache_chunks:
        ck_ref, k_ref, cvt_ref, vt_ref, o_ref, *scratch = rest
    else:
        k_ref, vt_ref, o_ref, *scratch = rest
    slots = (scratch[0::2], scratch[1::2])
    n_chunks = n_cache_chunks + n_self_chunks
    g = pl.program_id(0)

    def keys(c):
        if c < n_cache_chunks:
            return ck_ref[0, c * KEY_CHUNK:(c + 1) * KEY_CHUNK, :]
        c -= n_cache_chunks
        return k_ref[0, c * KEY_CHUNK:(c + 1) * KEY_CHUNK, :]

    def values_t(c):
        if c < n_cache_chunks:
            return cvt_ref[0, :, c * KEY_CHUNK:(c + 1) * KEY_CHUNK]
        c -= n_cache_chunks
        return vt_ref[0, :, c * KEY_CHUNK:(c + 1) * KEY_CHUNK]

    def fold_rows(x, op):
        return op(x.reshape(x.shape[0] // 8, 8, x.shape[1]), axis=0)

    def step(new, old):
        s_new, m_new, _, _ = new
        s_old, m_old, p_old, l_old = old
        _, _, p_fin, l_fin = new
        lp = lp_ref[...]
        lam = (jnp.exp(jnp.sum(lp[0:1] * lp[1:2], axis=-1, keepdims=True))
               - jnp.exp(jnp.sum(lp[2:3] * lp[3:4], axis=-1, keepdims=True)) + LAM_INIT)

        qt = q_ref[0]
        row = lax.broadcasted_iota(jnp.int32, qt.shape, 0)
        zero = jnp.zeros_like(qt)
        qq = jnp.concatenate([jnp.where(row < QK_HALF, qt, zero),
                              jnp.where(row >= QK_HALF, qt, zero)], axis=1)

        m_prev = m_old[...]
        stats = l_fin[...]
        ratio = stats[0:1].astype(BF16)
        inv_l0 = stats[1:2]

        m_run = jnp.full((8, 2 * tq), -jnp.inf, F32)
        l_run = jnp.zeros((8, 2 * tq), F32)
        acc = jnp.zeros((V_DIM, tq), F32)
        for c in range(n_chunks):
            s = jnp.dot(keys(c), qq, preferred_element_type=F32)
            s_new[c] = s
            m_run = jnp.maximum(m_run, fold_rows(s, jnp.max))

            p = jnp.exp2(s_old[c] - m_prev)
            l_run = l_run + fold_rows(p, jnp.sum)
            p_old[c] = p.astype(BF16)

            pf = p_fin[c]
            a = pf[:, :tq] - ratio * pf[:, tq:]
            acc = acc + jnp.dot(values_t(c), a, preferred_element_type=F32)

        m_new[...] = jnp.max(m_run, axis=0, keepdims=True)
        l = jnp.sum(l_run, axis=0, keepdims=True)
        l0 = l[:, :tq]
        l1 = l[:, tq:]
        l_old[...] = jnp.concatenate([lam * l0 / l1, 1.0 / l0], axis=0)

        o = (acc * inv_l0).T
        o_ref[0] = (_rms(o, sg_ref[...]) * (1.0 - LAM_INIT)).astype(BF16)

    @pl.when(g == 0)
    def _():
        for ref in slots[1][:2] + slots[0][2:]:
            ref[...] = jnp.zeros_like(ref)

    @pl.when(g % 2 == 0)
    def _():
        step(slots[0], slots[1])

    @pl.when(g % 2 == 1)
    def _():
        step(slots[1], slots[0])


def _diff_attention(lambda_params, subln_g, qt, k, vt, cache_k, cache_vt, tq):
    b, n, _ = k.shape
    n_self_chunks = n // KEY_CHUNK
    n_cache_chunks = 0 if cache_k is None else cache_k.shape[1] // KEY_CHUNK
    n_q = n // tq
    n_items = b * N_HEADS * n_q

    def item(g):
        g = jnp.clip(g, 0, n_items - 1)
        return g // (N_HEADS * n_q), (g // n_q) % N_HEADS, g % n_q

    def scored(g):
        return item(g)

    def finished(g):
        return item(g - 2)

    def head_rows(which):
        return lambda g: (which(g)[0], 0, which(g)[1])

    def head_cols(which):
        return lambda g: (which(g)[0], which(g)[1], 0)

    whole = lambda a: pl.BlockSpec(a.shape, lambda g: (0,) * a.ndim)
    in_specs = [whole(lambda_params), whole(subln_g),
                pl.BlockSpec((1, QK_DIM, tq), lambda g: (scored(g)[0], scored(g)[1], scored(g)[2]))]
    args = [lambda_params, subln_g, qt]
    if n_cache_chunks:
        in_specs.append(pl.BlockSpec((1, cache_k.shape[1], QK_DIM), head_rows(scored)))
        args.append(cache_k)
    in_specs.append(pl.BlockSpec((1, n, QK_DIM), head_rows(scored)))
    args.append(k)
    if n_cache_chunks:
        in_specs.append(pl.BlockSpec((1, V_DIM, cache_vt.shape[2]), head_cols(finished)))
        args.append(cache_vt)
    in_specs.append(pl.BlockSpec((1, V_DIM, n), head_cols(finished)))
    args.append(vt)
    n_chunks = n_cache_chunks + n_self_chunks
    per_slot = [pltpu.VMEM((n_chunks, KEY_CHUNK, 2 * tq), F32),
                pltpu.VMEM((1, 2 * tq), F32),
                pltpu.VMEM((n_chunks, KEY_CHUNK, 2 * tq), BF16),
                pltpu.VMEM((2, tq), F32)]
    return pl.pallas_call(
        functools.partial(_attn_kernel, tq=tq, n_cache_chunks=n_cache_chunks,
                          n_self_chunks=n_self_chunks),
        out_shape=jax.ShapeDtypeStruct((b, n, N_HEADS * V_DIM), BF16),
        grid=(n_items + 2,),
        in_specs=in_specs,
        out_specs=pl.BlockSpec((1, tq, V_DIM),
                               lambda g: (finished(g)[0], finished(g)[2], finished(g)[1])),
        scratch_shapes=[shape for shape in per_slot for _ in range(2)],
        compiler_params=_params("arbitrary"),
        name="diff_attention_cached" if n_cache_chunks else "diff_attention",
    )(*args)


def _post_kernel(x_ref, u_ref, at_ref, gp_ref, ga_ref, mod_ref, g_ref, wmix_ref, ps_ref,
                 wpb_ref, wab_ref, wo_ref, wr_ref, br_ref,
                 x1_ref, h2_ref, idx_ref, wt_ref, upad_ref, *, tm, n):
    d = D_MODEL
    j = pl.program_id(1)
    r0 = pl.multiple_of(j * tm, tm)
    halo = POOL_HALO

    upad_ref[halo:halo + tm, :] = u_ref[0, pl.ds(r0, tm), :]
    upad_ref[0:halo, :] = jnp.zeros((halo, POOL_WIDTH), F32)
    upad_ref[halo + tm:, :] = jnp.zeros((halo, POOL_WIDTH), F32)

    @pl.when(j > 0)
    def _():
        upad_ref[0:halo, :] = u_ref[0, pl.ds(pl.multiple_of(r0 - halo, halo), halo), :]

    @pl.when(j < pl.num_programs(1) - 1)
    def _():
        upad_ref[halo + tm:, :] = u_ref[0, pl.ds(pl.multiple_of(r0 + tm, halo), halo), :]

    t = r0 + lax.broadcasted_iota(jnp.int32, (tm, POOL_GROUP_DIM), 0)
    mixed = []
    for g, w in enumerate(POOL_WINDOWS):
        sl = slice(g * POOL_GROUP_DIM, (g + 1) * POOL_GROUP_DIM)
        window = upad_ref[halo - w // 2:halo - w // 2 + tm, sl]
        for off in range(-w // 2 + 1, w // 2):
            window = window + upad_ref[halo + off:halo + off + tm, sl]
        count = (jnp.minimum(t + w // 2, n) - jnp.maximum(t - w // 2, 0)).astype(F32)
        pooled = window / count - upad_ref[halo:halo + tm, sl]
        mixed.append(jnp.dot(pooled.astype(BF16), wmix_ref[g], preferred_element_type=F32))
    pool = (jnp.concatenate(mixed, axis=1) * ps_ref[...]).astype(BF16)

    pool_b = jnp.dot(pool, wpb_ref[...], preferred_element_type=F32)
    attn_b = jnp.dot(at_ref[0], wab_ref[...], preferred_element_type=F32)
    merged = (jax.nn.sigmoid(gp_ref[0].astype(F32)) * pool_b
              + jax.nn.sigmoid(ga_ref[0].astype(F32)) * attn_b)
    mix = jnp.dot(merged.astype(BF16), wo_ref[...], preferred_element_type=F32)

    mod = mod_ref[0]
    x1 = x_ref[0] + mod[:, 2 * d:3 * d] * _rms(mix, g_ref[1:2, :])
    x1_ref[0] = x1
    h2 = _rms(x1, g_ref[2:3, :]) * (1.0 + mod[:, 4 * d:5 * d]) + mod[:, 3 * d:4 * d]
    h2_ref[0] = h2

    logits = _dot3(h2, wr_ref[...]) + br_ref[...]
    lane = lax.broadcasted_iota(jnp.int32, logits.shape, 1).astype(F32)
    vals, idxs = [], []
    for _ in range(TOP_K):
        mx = jnp.max(logits, axis=-1, keepdims=True)
        first = jnp.min(jnp.where(logits == mx, lane, float(N_EXPERTS)), axis=-1, keepdims=True)
        vals.append(mx)
        idxs.append(first)
        logits = jnp.where(lane == first, -jnp.inf, logits)
    e = [jnp.exp(v - vals[0]) for v in vals]
    denom = e[0] + e[1] + e[2] + e[3]
    idx_ref[0] = jnp.concatenate(idxs, axis=1).astype(jnp.int32)
    wt_ref[0] = jnp.concatenate([ek / denom for ek in e], axis=1)


def _mixer_tail(x, u, attn, g_pool, g_attn, mod, mod_row, gains, w_pool_mix, pool_scale,
                w_pool_br, w_attn_br, w_out, w_router, b_router, tm):
    b, n, d = x.shape
    row_spec = lambda w: pl.BlockSpec((1, tm, w), lambda i, j: (i, j, 0))
    whole = lambda a: pl.BlockSpec(a.shape, lambda i, j: (0,) * a.ndim,
                                   pipeline_mode=pl.Buffered(1))
    in_specs = [row_spec(d),
                pl.BlockSpec((1, n, POOL_WIDTH), lambda i, j: (i, 0, 0)),
                row_spec(d), row_spec(d), row_spec(d),
                pl.BlockSpec((1, 1, mod.shape[-1]), lambda i, j: (mod_row(i), 0, 0)),
                whole(gains), whole(w_pool_mix), whole(pool_scale), whole(w_pool_br),
                whole(w_attn_br), whole(w_out), whole(w_router), whole(b_router)]
    return pl.pallas_call(
        functools.partial(_post_kernel, tm=tm, n=n),
        out_shape=[jax.ShapeDtypeStruct((b, n, d), F32),
                   jax.ShapeDtypeStruct((b, n, d), F32),
                   jax.ShapeDtypeStruct((b, n, TOP_K), jnp.int32),
                   jax.ShapeDtypeStruct((b, n, TOP_K), F32)],
        grid=(b, n // tm),
        in_specs=in_specs,
        out_specs=[row_spec(d), row_spec(d), row_spec(TOP_K), row_spec(TOP_K)],
        scratch_shapes=[pltpu.VMEM((tm + 2 * POOL_HALO, POOL_WIDTH), F32)],
        compiler_params=_params("arbitrary", "arbitrary"),
        name="mixer_tail",
    )(x, u, attn, g_pool, g_attn, mod, gains, w_pool_mix, pool_scale, w_pool_br, w_attn_br,
      w_out, w_router, b_router)


def _start_row_copies(make_copy):
    def issue(blk, carry):
        for r in range(DMA_UNROLL):
            for k in range(TOP_K):
                make_copy(blk * DMA_UNROLL + r, k).start(priority=(r * TOP_K + k) % 2)
        return carry

    lax.fori_loop(0, ROW_BLOCK // DMA_UNROLL, issue, 0)


def _wait_row_copies(one_copy):
    def drain(blk, carry):
        for _ in range(DMA_UNROLL * TOP_K):
            one_copy.wait()
        return carry

    lax.fori_loop(0, ROW_BLOCK // DMA_UNROLL, drain, 0)


def _dispatch_kernel(fill_ref, pos_ref, hc_ref, hs_ref, xs_ref, zero_ref, sems, fill_sem, *,
                     ctx_blocks):
    i = pl.program_id(0)
    last = pl.num_programs(0) - 1
    slot = i % 2

    @pl.when(i == 0)
    def _():
        zero_ref[...] = jnp.zeros_like(zero_ref)
        n_fill = fill_ref.shape[1]

        def fill_copy(j):
            row = pl.multiple_of(fill_ref[0, j] * EXPERT_TILE, EXPERT_TILE)
            return pltpu.make_async_copy(zero_ref, xs_ref.at[pl.ds(row, EXPERT_TILE)], fill_sem)

        for j in range(n_fill):
            pl.when(fill_ref[1, j] > 0)(lambda j=j: fill_copy(j).start())
        for j in range(n_fill):
            pl.when(fill_ref[1, j] > 0)(lambda j=j: fill_copy(j).wait())

    def row_copy(h_ref, src_row, dst_row, sem):
        return pltpu.make_async_copy(h_ref.at[pl.ds(src_row, 1)], xs_ref.at[pl.ds(dst_row, 1)], sem)

    def start_block(h_ref, first_row):
        _start_row_copies(lambda r, k: row_copy(h_ref, first_row + r, pos_ref[0, 0, r * TOP_K + k],
                                                sems.at[slot]))

    @pl.when(i < ctx_blocks)
    def _():
        start_block(hc_ref, i * ROW_BLOCK)

    @pl.when(i >= ctx_blocks)
    def _():
        start_block(hs_ref, (i - ctx_blocks) * ROW_BLOCK)

    @pl.when(i > 0)
    def _():
        _wait_row_copies(row_copy(hc_ref, 0, 0, sems.at[1 - slot]))

    @pl.when(i == last)
    def _():
        _wait_row_copies(row_copy(hc_ref, 0, 0, sems.at[slot]))


def _dispatch(fill_tiles, pos, h2_c, h2_s, n_rows):
    d = h2_c.shape[1]
    ctx_blocks = h2_c.shape[0] // ROW_BLOCK
    n_blk = pos.shape[0] // ROW_BLOCK
    any_space = pl.BlockSpec(memory_space=pl.ANY)
    return pl.pallas_call(
        functools.partial(_dispatch_kernel, ctx_blocks=ctx_blocks),
        out_shape=jax.ShapeDtypeStruct((n_rows, d), F32),
        grid_spec=pltpu.PrefetchScalarGridSpec(
            num_scalar_prefetch=1, grid=(n_blk,),
            in_specs=[pl.BlockSpec((1, 1, ROW_BLOCK * TOP_K), lambda i, ft: (i, 0, 0),
                                   memory_space=pltpu.SMEM),
                      any_space, any_space],
            out_specs=any_space,
            scratch_shapes=[pltpu.VMEM((EXPERT_TILE, d), F32), pltpu.SemaphoreType.DMA((2,)),
                            pltpu.SemaphoreType.DMA(())]),
        compiler_params=_params("arbitrary"),
        name="expert_dispatch",
    )(fill_tiles, pos.reshape(n_blk, 1, ROW_BLOCK * TOP_K), h2_c, h2_s)


def _moe_kernel(te_ref, nt_ref, x_ref, wi_ref, bi_ref, wo_ref, bo_ref, y_ref):
    i = pl.program_id(0)

    @pl.when(i < nt_ref[0])
    def _():
        gu = jnp.dot(x_ref[...].astype(BF16), wi_ref[0].astype(BF16),
                     preferred_element_type=F32) + bi_ref[0]
        g_lin = jnp.minimum(gu[:, :D_EXPERT], SWIGLU_LIMIT)
        u_lin = jnp.clip(gu[:, D_EXPERT:], -SWIGLU_LIMIT, SWIGLU_LIMIT)
        hid = (u_lin + 1.0) * (g_lin * jax.nn.sigmoid(SWIGLU_ALPHA * g_lin))
        y_ref[...] = jnp.dot(hid.astype(BF16), wo_ref[0].astype(BF16),
                             preferred_element_type=F32) + bo_ref[0]

    @pl.when(i >= nt_ref[0])
    def _():
        y_ref[...] = jnp.zeros_like(y_ref)


def _grouped_mlp(tile_expert, n_tiles, xs, w_in, b_in, w_out, b_out):
    p, d = xs.shape
    max_tiles = p // EXPERT_TILE

    def active(i, te, nt):
        return jnp.minimum(i, nt[0] - 1)

    def expert(i, te, nt):
        return te[active(i, te, nt)]

    grid_spec = pltpu.PrefetchScalarGridSpec(
        num_scalar_prefetch=2,
        grid=(max_tiles,),
        in_specs=[pl.BlockSpec((EXPERT_TILE, d), lambda i, te, nt: (active(i, te, nt), 0)),
                  pl.BlockSpec((1, d, 2 * D_EXPERT), lambda i, te, nt: (expert(i, te, nt), 0, 0)),
                  pl.BlockSpec((1, 1, 2 * D_EXPERT), lambda i, te, nt: (expert(i, te, nt), 0, 0)),
                  pl.BlockSpec((1, D_EXPERT, d), lambda i, te, nt: (expert(i, te, nt), 0, 0)),
                  pl.BlockSpec((1, 1, d), lambda i, te, nt: (expert(i, te, nt), 0, 0))],
        out_specs=pl.BlockSpec((EXPERT_TILE, d), lambda i, te, nt: (i, 0)))
    return pl.pallas_call(
        _moe_kernel,
        out_shape=jax.ShapeDtypeStruct((p, d), F32),
        grid_spec=grid_spec,
        compiler_params=_params("arbitrary"),
        name="grouped_expert_mlp",
    )(tile_expert, n_tiles, xs, w_in, b_in, w_out, b_out)


def _combine_kernel(pos_ref, next_pos_ref, wt_ref, x1_ref, mod_ref, g_ref, ys_ref, o_ref,
                    buf_ref, sems):
    d = D_MODEL
    i = pl.program_id(0)
    slot = i % 2

    def row_copy(src_row, dst_slot, k, r):
        return pltpu.make_async_copy(ys_ref.at[pl.ds(src_row, 1)],
                                     buf_ref.at[dst_slot, k, pl.ds(r, 1)], sems.at[dst_slot])

    @pl.when(i == 0)
    def _():
        _start_row_copies(lambda r, k: row_copy(pos_ref[0, 0, r * TOP_K + k], slot, k, r))

    @pl.when(i + 1 < pl.num_programs(0))
    def _():
        _start_row_copies(lambda r, k: row_copy(next_pos_ref[0, 0, r * TOP_K + k], 1 - slot, k, r))

    _wait_row_copies(row_copy(0, slot, 0, 0))
    wt = wt_ref[...]
    moe = wt[:, 0:1] * buf_ref[slot, 0]
    for k in range(1, TOP_K):
        moe = moe + wt[:, k:k + 1] * buf_ref[slot, k]
    mod = mod_ref[0]
    o_ref[...] = x1_ref[...] + mod[:, 5 * d:6 * d] * _rms(moe, g_ref[3:4, :])


def _combine(pos, wt, x1, mod, mod_row, gains, ys):
    t, d = x1.shape
    n_blk = t // ROW_BLOCK
    pos = pos.reshape(n_blk, 1, ROW_BLOCK * TOP_K)
    pos_block = lambda index: pl.BlockSpec((1, 1, ROW_BLOCK * TOP_K), lambda i: (index(i), 0, 0),
                                           memory_space=pltpu.SMEM)
    return pl.pallas_call(
        _combine_kernel,
        out_shape=jax.ShapeDtypeStruct((t, d), F32),
        grid=(n_blk,),
        in_specs=[pos_block(lambda i: i),
                  pos_block(lambda i: jnp.minimum(i + 1, n_blk - 1)),
                  pl.BlockSpec((ROW_BLOCK, TOP_K), lambda i: (i, 0)),
                  pl.BlockSpec((ROW_BLOCK, d), lambda i: (i, 0)),
                  pl.BlockSpec((1, 1, mod.shape[-1]), lambda i: (mod_row(i), 0, 0)),
                  pl.BlockSpec(gains.shape, lambda i: (0, 0)),
                  pl.BlockSpec(memory_space=pl.ANY)],
        out_specs=pl.BlockSpec((ROW_BLOCK, d), lambda i: (i, 0)),
        scratch_shapes=[pltpu.VMEM((2, TOP_K, ROW_BLOCK, d), F32), pltpu.SemaphoreType.DMA((2,))],
        compiler_params=_params("arbitrary"),
        name=f"expert_combine_{n_blk}",
    )(pos, pos, wt, x1, mod, gains, ys)


def _routing_tables(idx):
    t = idx.shape[0]
    onehot = idx[:, :, None] == jnp.arange(N_EXPERTS, dtype=jnp.int32)
    mask = jnp.any(onehot, axis=1).astype(jnp.int32)
    csum = jnp.cumsum(mask, axis=0)
    rank = csum - mask
    counts = csum[-1]
    tiles = (counts + EXPERT_TILE - 1) // EXPERT_TILE
    tile_end = jnp.cumsum(tiles)
    row_start = (tile_end - tiles) * EXPERT_TILE
    dest = row_start[None, :] + rank
    pos = jnp.sum(jnp.where(onehot, dest[:, None, :], 0), axis=-1)
    max_tiles = (t * TOP_K) // EXPERT_TILE + N_EXPERTS
    tile_expert = jnp.sum(jnp.arange(max_tiles, dtype=jnp.int32)[:, None] >= tile_end[None, :],
                          axis=-1)
    tile_expert = jnp.minimum(tile_expert, N_EXPERTS - 1).astype(jnp.int32)
    unused = tile_end[-1] + jnp.arange(N_EXPERTS, dtype=tile_end.dtype)
    fill_tile = jnp.concatenate([tile_end - 1, unused])
    fill_live = jnp.concatenate([tiles > 0, unused < max_tiles])
    fill_tiles = jnp.stack([jnp.clip(fill_tile, 0, max_tiles - 1), fill_live.astype(fill_tile.dtype)])
    return (pos.astype(jnp.int32), tile_expert, tile_end[-1:].astype(jnp.int32),
            fill_tiles.astype(jnp.int32), max_tiles)


def kernel(x_prompt, x_sample, c, cache_k, cache_v, c_ctx, w_ada, b_ada, norm_gains, w_in,
           w_pool_mix, pool_scale, lambda_params, subln_g, w_pool_br, w_attn_br, w_out,
           w_router, b_router, w_moe_in, b_moe_in, w_moe_out, b_moe_out):
    depth = w_ada.shape[0]
    assert depth == 1, "one trunk layer"
    d = D_MODEL
    nb, ns, _ = x_prompt.shape
    db, dn, _ = x_sample.shape
    past = cache_k.shape[2]
    l = 0

    ctx_row = db
    mod_rows = -(-(db + 1) // 8) * 8
    cc = jnp.zeros((mod_rows, d), F32).at[:db].set(c).at[ctx_row].set(c_ctx)
    mod = _modulation(cc, w_ada[l], b_ada[l][None, :])[:, None, :]

    gains = norm_gains[l]
    w_in_b = w_in[l].astype(BF16)
    w_mix_b = w_pool_mix[l].astype(BF16)
    w_pb_b = w_pool_br[l].astype(BF16)
    w_ab_b = w_attn_br[l].astype(BF16)
    w_o_b = w_out[l].astype(BF16)
    ps = pool_scale[l][None, :]
    sg = subln_g[l][None, :]
    lp = lambda_params[l]
    wr = w_router[l]
    br = b_router[l][None, :]
    ck = cache_k[:, l].reshape(db, past, N_HEADS * QK_DIM).astype(BF16)
    cv = cache_v[:, l].reshape(db, past, N_HEADS * V_DIM).astype(BF16).transpose(0, 2, 1)

    ctx_mod = lambda i: ctx_row
    lat_mod = lambda i: i

    def mixer(x, mod_row, rope_tables, cache, tm, tq, keep_kv):
        outs = _in_projection(x, mod, mod_row, gains, w_in_b, rope_tables, tm, keep_kv)
        u, qt, k, vt, gp, ga = outs[:6]
        attn = _diff_attention(lp, sg, qt, k, vt, cache[0], cache[1], tq)
        x1, h2, idx, wt = _mixer_tail(x, u, attn, gp, ga, mod, mod_row, gains, w_mix_b, ps,
                                      w_pb_b, w_ab_b, w_o_b, wr, br, min(tm, TAIL_BLOCK))
        return x1, h2, idx, wt, outs[6:]

    x1_c, h2_c, idx_c, wt_c, (new_k, new_v) = mixer(x_prompt, ctx_mod, None, (None, None),
                                                    min(ns, 256), min(ns, 256), True)
    x1_s, h2_s, idx_s, wt_s, _ = mixer(x_sample, lat_mod, _rope_tables(dn), (ck, cv),
                                       512, 256, False)

    t_c = nb * ns
    t_s = db * dn
    idx_all = jnp.concatenate([idx_c.reshape(t_c, TOP_K), idx_s.reshape(t_s, TOP_K)], axis=0)
    pos, tile_expert, n_tiles, fill_tiles, max_tiles = _routing_tables(idx_all)

    n_rows = max_tiles * EXPERT_TILE
    xs = _dispatch(fill_tiles, pos, h2_c.reshape(t_c, d), h2_s.reshape(t_s, d), n_rows)
    ys = _grouped_mlp(tile_expert, n_tiles, xs, w_moe_in[l], b_moe_in[l][:, None, :],
                      w_moe_out[l], b_moe_out[l][:, None, :])

    y_c = _combine(pos[:t_c], wt_c.reshape(t_c, TOP_K), x1_c.reshape(t_c, d), mod,
                   ctx_mod, gains, ys)
    blocks_per_request = dn // ROW_BLOCK
    y_s = _combine(pos[t_c:], wt_s.reshape(t_s, TOP_K), x1_s.reshape(t_s, d), mod,
                   lambda i: i // blocks_per_request, gains, ys)

    return (y_c.reshape(nb, ns, d), y_s.reshape(db, dn, d),
            new_k.reshape(nb, 1, ns, N_HEADS, QK_DIM), new_v.reshape(nb, 1, ns, N_HEADS, V_DIM))
```

```python
import functools
import math

import jax
import jax.numpy as jnp
from jax import lax
from jax.experimental import pallas as pl
from jax.experimental.pallas import tpu as pltpu

D_MODEL = 1024
GRID_W = 64
N_HEADS = 8
QK_HALF = 64
QK_DIM = 2 * QK_HALF
V_DIM = 128
ROPE_BASE = 10000.0
POOL_WINDOWS = (2, 4, 8, 16)
POOL_GROUP_DIM = 128
POOL_WIDTH = len(POOL_WINDOWS) * POOL_GROUP_DIM
POOL_HALO = max(POOL_WINDOWS) // 2
N_EXPERTS = 32
TOP_K = 4
D_EXPERT = 512
SWIGLU_ALPHA = 1.702
SWIGLU_LIMIT = 7.0
RMS_EPS = 1e-6
LAM_INIT = 0.8 - 0.6 * math.exp(-0.3 * 0)

V7X_LANES = 128
V7X_VMEM_LIMIT_BYTES = 56 * 1024 * 1024

COL_CHUNK = 512
KEY_CHUNK = 256
Q_SCALE = QK_HALF ** -0.5 * math.log2(math.e)
TAIL_BLOCK = 256
EXPERT_TILE = 512
ROW_BLOCK = 256

BF16 = jnp.bfloat16
F32 = jnp.float32


def _params(*semantics):
    return pltpu.CompilerParams(dimension_semantics=semantics,
                                vmem_limit_bytes=V7X_VMEM_LIMIT_BYTES)


def _split_bf16(x):
    hi = x.astype(BF16)
    lo = (x - hi.astype(F32)).astype(BF16)
    return hi, lo


def _dot3(x, w):
    xh, xl = _split_bf16(x)
    wh, wl = _split_bf16(w)
    dot = functools.partial(jnp.dot, preferred_element_type=F32)
    return dot(xh, wh) + (dot(xl, wh) + dot(xh, wl))


def _rms(x, gain):
    return x * lax.rsqrt(jnp.mean(x * x, axis=-1, keepdims=True) + RMS_EPS) * gain


def _ada_kernel(c_ref, w_ref, b_ref, o_ref):
    c = c_ref[...]
    o_ref[...] = _dot3(c * jax.nn.sigmoid(c), w_ref[...]) + b_ref[...]


def _modulation(cc, w_ada, b_ada):
    rows, d = cc.shape
    n = w_ada.shape[1]
    return pl.pallas_call(
        _ada_kernel,
        out_shape=jax.ShapeDtypeStruct((rows, n), F32),
        grid=(n // COL_CHUNK,),
        in_specs=[pl.BlockSpec((rows, d), lambda j: (0, 0)),
                  pl.BlockSpec((d, COL_CHUNK), lambda j: (0, j)),
                  pl.BlockSpec((1, COL_CHUNK), lambda j: (0, j))],
        out_specs=pl.BlockSpec((rows, COL_CHUNK), lambda j: (0, j)),
        compiler_params=_params("arbitrary"),
        name="modulation",
    )(cc, w_ada, b_ada)


def _rope_slab(x, cos, sin_signed, first_half):
    partner = jnp.where(first_half, pltpu.roll(x, V7X_LANES - QK_HALF // 2, 1),
                        pltpu.roll(x, QK_HALF // 2, 1))
    return x * cos + partner * sin_signed


def _inproj_kernel(x_ref, mod_ref, g_ref, w_ref, *rest, rope, keep_kv):
    if rope:
        cos_ref, sin_ref, *outs = rest
    else:
        outs = list(rest)
    if keep_kv:
        u_ref, qt_ref, k_ref, vt_ref, gp_ref, ga_ref, nk_ref, nv_ref = outs
    else:
        u_ref, qt_ref, k_ref, vt_ref, gp_ref, ga_ref = outs
    d = D_MODEL
    x = x_ref[0]
    mod = mod_ref[0]
    h = _rms(x, g_ref[0:1, :]) * (1.0 + mod[:, d:2 * d]) + mod[:, 0:d]
    hb = h.astype(BF16)
    if rope:
        cos = cos_ref[...]
        sin = sin_ref[...]
        lane = lax.broadcasted_iota(jnp.int32, cos.shape, 1)
        first_half = (lane % QK_HALF) < (QK_HALF // 2)

    def proj(col):
        return jnp.dot(hb, w_ref[:, col:col + COL_CHUNK], preferred_element_type=F32)

    def rotate(r):
        if not rope:
            return r
        slabs = [_rope_slab(r[:, s:s + V7X_LANES], cos, sin, first_half)
                 for s in range(0, COL_CHUNK, V7X_LANES)]
        return jnp.concatenate(slabs, axis=1)

    u_ref[0] = proj(0)
    q0 = POOL_WIDTH
    k0 = q0 + d
    v0 = k0 + d
    gp0 = v0 + d
    ga0 = gp0 + d
    for c in range(0, d, COL_CHUNK):
        qt_ref[0, c:c + COL_CHUNK, :] = (rotate(proj(q0 + c)) * Q_SCALE).T.astype(BF16)
        kc = proj(k0 + c)
        k_ref[0, :, c:c + COL_CHUNK] = rotate(kc).astype(BF16)
        vc = proj(v0 + c)
        vt_ref[0, c:c + COL_CHUNK, :] = vc.T.astype(BF16)
        if keep_kv:
            nk_ref[0, :, c:c + COL_CHUNK] = kc
            nv_ref[0, :, c:c + COL_CHUNK] = vc
        gp_ref[0, :, c:c + COL_CHUNK] = proj(gp0 + c).astype(BF16)
        ga_ref[0, :, c:c + COL_CHUNK] = proj(ga0 + c).astype(BF16)


def _in_projection(x, mod, mod_row, gains, w_in, rope_tables, tm, keep_kv):
    b, n, d = x.shape
    rope = rope_tables is not None
    row_spec = lambda w: pl.BlockSpec((1, tm, w), lambda i, j: (i, j, 0))
    in_specs = [row_spec(d),
                pl.BlockSpec((1, 1, mod.shape[-1]), lambda i, j: (mod_row(i), 0, 0)),
                pl.BlockSpec(gains.shape, lambda i, j: (0, 0)),
                pl.BlockSpec(w_in.shape, lambda i, j: (0, 0), pipeline_mode=pl.Buffered(1))]
    args = [x, mod, gains, w_in]
    if rope:
        in_specs += [pl.BlockSpec((tm, V7X_LANES), lambda i, j: (j, 0))] * 2
        args += list(rope_tables)
    token_major = lambda w, t: (jax.ShapeDtypeStruct((b, n, w), t), row_spec(w))
    feature_major = (jax.ShapeDtypeStruct((b, d, n), BF16),
                     pl.BlockSpec((1, d, tm), lambda i, j: (i, 0, j)))
    outs = [token_major(POOL_WIDTH, F32), feature_major, token_major(d, BF16), feature_major,
            token_major(d, BF16), token_major(d, BF16)]
    if keep_kv:
        outs += [token_major(d, F32)] * 2
    return pl.pallas_call(
        functools.partial(_inproj_kernel, rope=rope, keep_kv=keep_kv),
        out_shape=[shape for shape, _ in outs],
        grid=(b, n // tm),
        in_specs=in_specs,
        out_specs=[spec for _, spec in outs],
        compiler_params=_params("arbitrary", "arbitrary"),
        name="in_projection_rope" if rope else "in_projection",
    )(*args)


def _rope_tables(n):
    rows = n // GRID_W
    row = jnp.repeat(jnp.arange(rows), GRID_W).astype(F32)
    col = jnp.tile(jnp.arange(GRID_W), rows).astype(F32)
    n_freq = QK_HALF // 4
    inv = ROPE_BASE ** (-jnp.arange(n_freq, dtype=F32) / n_freq)
    ang = jnp.concatenate([row[:, None] * inv, col[:, None] * inv], axis=-1)
    cos = jnp.cos(ang)
    sin = jnp.sin(ang)
    unit_cos = jnp.concatenate([cos, cos], axis=-1)
    unit_sin = jnp.concatenate([-sin, sin], axis=-1)
    reps = V7X_LANES // QK_HALF
    return jnp.tile(unit_cos, (1, reps)), jnp.tile(unit_sin, (1, reps))


def _attn_kernel(lp_ref, sg_ref, q_ref, *rest, tq, n_cache_chunks, n_self_chunks):
    if n_cache_chunks:
        ck_ref, k_ref, cvt_ref, vt_ref, o_ref, *scratch = rest
    else:
        k_ref, vt_ref, o_ref, *scratch = rest
    slots = (scratch[0::2], scratch[1::2])
    n_chunks = n_cache_chunks + n_self_chunks
    g = pl.program_id(0)

    def keys(c):
        if c < n_cache_chunks:
            return ck_ref[0, c * KEY_CHUNK:(c + 1) * KEY_CHUNK, :]
        c -= n_cache_chunks
        return k_ref[0, c * KEY_CHUNK:(c + 1) * KEY_CHUNK, :]

    def values_t(c):
        if c < n_cache_chunks:
            return cvt_ref[0, :, c * KEY_CHUNK:(c + 1) * KEY_CHUNK]
        c -= n_cache_chunks
        return vt_ref[0, :, c * KEY_CHUNK:(c + 1) * KEY_CHUNK]

    def fold_rows(x, op):
        return op(x.reshape(x.shape[0] // 8, 8, x.shape[1]), axis=0)

    def step(new, old):
        s_new, m_new, _, _ = new
        s_old, m_old, p_old, l_old = old
        _, _, p_fin, l_fin = new
        lp = lp_ref[...]
        lam = (jnp.exp(jnp.sum(lp[0:1] * lp[1:2], axis=-1, keepdims=True))
               - jnp.exp(jnp.sum(lp[2:3] * lp[3:4], axis=-1, keepdims=True)) + LAM_INIT)

        qt = q_ref[0]
        row = lax.broadcasted_iota(jnp.int32, qt.shape, 0)
        zero = jnp.zeros_like(qt)
        qq = jnp.concatenate([jnp.where(row < QK_HALF, qt, zero),
                              jnp.where(row >= QK_HALF, qt, zero)], axis=1)

        m_prev = m_old[...]
        stats = l_fin[...]
        ratio = stats[0:1].astype(BF16)
        inv_l0 = stats[1:2]

        m_run = jnp.full((8, 2 * tq), -jnp.inf, F32)
        l_run = jnp.zeros((8, 2 * tq), F32)
        acc = jnp.zeros((V_DIM, tq), F32)
        for c in range(n_chunks):
            s = jnp.dot(keys(c), qq, preferred_element_type=F32)
            s_new[c] = s
            m_run = jnp.maximum(m_run, fold_rows(s, jnp.max))

            p = jnp.exp2(s_old[c] - m_prev)
            l_run = l_run + fold_rows(p, jnp.sum)
            p_old[c] = p.astype(BF16)

            pf = p_fin[c]
            a = pf[:, :tq] - ratio * pf[:, tq:]
            acc = acc + jnp.dot(values_t(c), a, preferred_element_type=F32)

        m_new[...] = jnp.max(m_run, axis=0, keepdims=True)
        l = jnp.sum(l_run, axis=0, keepdims=True)
        l0 = l[:, :tq]
        l1 = l[:, tq:]
        l_old[...] = jnp.concatenate([lam * l0 / l1, 1.0 / l0], axis=0)

        o = (acc * inv_l0).T
        o_ref[0] = (_rms(o, sg_ref[...]) * (1.0 - LAM_INIT)).astype(BF16)

    @pl.when(g == 0)
    def _():
        for ref in slots[1][:2] + slots[0][2:]:
            ref[...] = jnp.zeros_like(ref)

    @pl.when(g % 2 == 0)
    def _():
        step(slots[0], slots[1])

    @pl.when(g % 2 == 1)
    def _():
        step(slots[1], slots[0])


def _diff_attention(lambda_params, subln_g, qt, k, vt, cache_k, cache_vt, tq):
    b, n, _ = k.shape
    n_self_chunks = n // KEY_CHUNK
    n_cache_chunks = 0 if cache_k is None else cache_k.shape[1] // KEY_CHUNK
    n_q = n // tq
    n_items = b * N_HEADS * n_q

    def item(g):
        g = jnp.clip(g, 0, n_items - 1)
        return g // (N_HEADS * n_q), (g // n_q) % N_HEADS, g % n_q

    def scored(g):
        return item(g)

    def finished(g):
        return item(g - 2)

    def head_rows(which):
        return lambda g: (which(g)[0], 0, which(g)[1])

    def head_cols(which):
        return lambda g: (which(g)[0], which(g)[1], 0)

    whole = lambda a: pl.BlockSpec(a.shape, lambda g: (0,) * a.ndim)
    in_specs = [whole(lambda_params), whole(subln_g),
                pl.BlockSpec((1, QK_DIM, tq), lambda g: (scored(g)[0], scored(g)[1], scored(g)[2]))]
    args = [lambda_params, subln_g, qt]
    if n_cache_chunks:
        in_specs.append(pl.BlockSpec((1, cache_k.shape[1], QK_DIM), head_rows(scored)))
        args.append(cache_k)
    in_specs.append(pl.BlockSpec((1, n, QK_DIM), head_rows(scored)))
    args.append(k)
    if n_cache_chunks:
        in_specs.append(pl.BlockSpec((1, V_DIM, cache_vt.shape[2]), head_cols(finished)))
        args.append(cache_vt)
    in_specs.append(pl.BlockSpec((1, V_DIM, n), head_cols(finished)))
    args.append(vt)
    n_chunks = n_cache_chunks + n_self_chunks
    per_slot = [pltpu.VMEM((n_chunks, KEY_CHUNK, 2 * tq), F32),
                pltpu.VMEM((1, 2 * tq), F32),
                pltpu.VMEM((n_chunks, KEY_CHUNK, 2 * tq), BF16),
                pltpu.VMEM((2, tq), F32)]
    return pl.pallas_call(
        functools.partial(_attn_kernel, tq=tq, n_cache_chunks=n_cache_chunks,
                          n_self_chunks=n_self_chunks),
        out_shape=jax.ShapeDtypeStruct((b, n, N_HEADS * V_DIM), BF16),
        grid=(n_items + 2,),
        in_specs=in_specs,
        out_specs=pl.BlockSpec((1, tq, V_DIM),
                               lambda g: (finished(g)[0], finished(g)[2], finished(g)[1])),
        scratch_shapes=[shape for shape in per_slot for _ in range(2)],
        compiler_params=_params("arbitrary"),
        name="diff_attention_cached" if n_cache_chunks else "diff_attention",
    )(*args)


def _post_kernel(x_ref, u_ref, at_ref, gp_ref, ga_ref, mod_ref, g_ref, wmix_ref, ps_ref,
                 wpb_ref, wab_ref, wo_ref, wr_ref, br_ref,
                 x1_ref, h2_ref, idx_ref, wt_ref, upad_ref, *, tm, n):
    d = D_MODEL
    j = pl.program_id(1)
    r0 = pl.multiple_of(j * tm, tm)
    halo = POOL_HALO

    upad_ref[halo:halo + tm, :] = u_ref[0, pl.ds(r0, tm), :]
    upad_ref[0:halo, :] = jnp.zeros((halo, POOL_WIDTH), F32)
    upad_ref[halo + tm:, :] = jnp.zeros((halo, POOL_WIDTH), F32)

    @pl.when(j > 0)
    def _():
        upad_ref[0:halo, :] = u_ref[0, pl.ds(pl.multiple_of(r0 - halo, halo), halo), :]

    @pl.when(j < pl.num_programs(1) - 1)
    def _():
        upad_ref[halo + tm:, :] = u_ref[0, pl.ds(pl.multiple_of(r0 + tm, halo), halo), :]

    t = r0 + lax.broadcasted_iota(jnp.int32, (tm, POOL_GROUP_DIM), 0)
    mixed = []
    for g, w in enumerate(POOL_WINDOWS):
        sl = slice(g * POOL_GROUP_DIM, (g + 1) * POOL_GROUP_DIM)
        window = upad_ref[halo - w // 2:halo - w // 2 + tm, sl]
        for off in range(-w // 2 + 1, w // 2):
            window = window + upad_ref[halo + off:halo + off + tm, sl]
        count = (jnp.minimum(t + w // 2, n) - jnp.maximum(t - w // 2, 0)).astype(F32)
        pooled = window / count - upad_ref[halo:halo + tm, sl]
        mixed.append(jnp.dot(pooled.astype(BF16), wmix_ref[g], preferred_element_type=F32))
    pool = (jnp.concatenate(mixed, axis=1) * ps_ref[...]).astype(BF16)

    pool_b = jnp.dot(pool, wpb_ref[...], preferred_element_type=F32)
    attn_b = jnp.dot(at_ref[0], wab_ref[...], preferred_element_type=F32)
    merged = (jax.nn.sigmoid(gp_ref[0].astype(F32)) * pool_b
              + jax.nn.sigmoid(ga_ref[0].astype(F32)) * attn_b)
    mix = jnp.dot(merged.astype(BF16), wo_ref[...], preferred_element_type=F32)

    mod = mod_ref[0]
    x1 = x_ref[0] + mod[:, 2 * d:3 * d] * _rms(mix, g_ref[1:2, :])
    x1_ref[0] = x1
    h2 = _rms(x1, g_ref[2:3, :]) * (1.0 + mod[:, 4 * d:5 * d]) + mod[:, 3 * d:4 * d]
    h2_ref[0] = h2

    logits = _dot3(h2, wr_ref[...]) + br_ref[...]
    lane = lax.broadcasted_iota(jnp.int32, logits.shape, 1).astype(F32)
    vals, idxs = [], []
    for _ in range(TOP_K):
        mx = jnp.max(logits, axis=-1, keepdims=True)
        first = jnp.min(jnp.where(logits == mx, lane, float(N_EXPERTS)), axis=-1, keepdims=True)
        vals.append(mx)
        idxs.append(first)
        logits = jnp.where(lane == first, -jnp.inf, logits)
    e = [jnp.exp(v - vals[0]) for v in vals]
    denom = e[0] + e[1] + e[2] + e[3]
    idx_ref[0] = jnp.concatenate(idxs, axis=1).astype(jnp.int32)
    wt_ref[0] = jnp.concatenate([ek / denom for ek in e], axis=1)


def _mixer_tail(x, u, attn, g_pool, g_attn, mod, mod_row, gains, w_pool_mix, pool_scale,
                w_pool_br, w_attn_br, w_out, w_router, b_router, tm):
    b, n, d = x.shape
    row_spec = lambda w: pl.BlockSpec((1, tm, w), lambda i, j: (i, j, 0))
    whole = lambda a: pl.BlockSpec(a.shape, lambda i, j: (0,) * a.ndim,
                                   pipeline_mode=pl.Buffered(1))
    in_specs = [row_spec(d),
                pl.BlockSpec((1, n, POOL_WIDTH), lambda i, j: (i, 0, 0)),
                row_spec(d), row_spec(d), row_spec(d),
                pl.BlockSpec((1, 1, mod.shape[-1]), lambda i, j: (mod_row(i), 0, 0)),
                whole(gains), whole(w_pool_mix), whole(pool_scale), whole(w_pool_br),
                whole(w_attn_br), whole(w_out), whole(w_router), whole(b_router)]
    return pl.pallas_call(
        functools.partial(_post_kernel, tm=tm, n=n),
        out_shape=[jax.ShapeDtypeStruct((b, n, d), F32),
                   jax.ShapeDtypeStruct((b, n, d), F32),
                   jax.ShapeDtypeStruct((b, n, TOP_K), jnp.int32),
                   jax.ShapeDtypeStruct((b, n, TOP_K), F32)],
        grid=(b, n // tm),
        in_specs=in_specs,
        out_specs=[row_spec(d), row_spec(d), row_spec(TOP_K), row_spec(TOP_K)],
        scratch_shapes=[pltpu.VMEM((tm + 2 * POOL_HALO, POOL_WIDTH), F32)],
        compiler_params=_params("arbitrary", "arbitrary"),
        name="mixer_tail",
    )(x, u, attn, g_pool, g_attn, mod, gains, w_pool_mix, pool_scale, w_pool_br, w_attn_br,
      w_out, w_router, b_router)


def _start_row_copies(make_copy):
    for r in range(ROW_BLOCK):
        for k in range(TOP_K):
            make_copy(r, k).start(priority=(r * TOP_K + k) % 2)


def _wait_row_copies(one_copy):
    for _ in range(ROW_BLOCK * TOP_K):
        one_copy.wait()


def _dispatch_kernel(fill_ref, pos_ref, hc_ref, hs_ref, xs_ref, zero_ref, sem, fill_sem, *,
                     ctx_blocks):
    i = pl.program_id(0)

    @pl.when(i == 0)
    def _():
        zero_ref[...] = jnp.zeros_like(zero_ref)
        n_fill = fill_ref.shape[1]

        def fill_copy(j):
            row = pl.multiple_of(fill_ref[0, j] * EXPERT_TILE, EXPERT_TILE)
            return pltpu.make_async_copy(zero_ref, xs_ref.at[pl.ds(row, EXPERT_TILE)], fill_sem)

        for j in range(n_fill):
            pl.when(fill_ref[1, j] > 0)(lambda j=j: fill_copy(j).start())
        for j in range(n_fill):
            pl.when(fill_ref[1, j] > 0)(lambda j=j: fill_copy(j).wait())

    def row_copy(h_ref, src_row, dst_row):
        return pltpu.make_async_copy(h_ref.at[pl.ds(src_row, 1)], xs_ref.at[pl.ds(dst_row, 1)], sem)

    def move_block(h_ref):
        _start_row_copies(lambda r, k: row_copy(h_ref, r, pos_ref[0, 0, r * TOP_K + k]))
        _wait_row_copies(row_copy(h_ref, 0, 0))

    @pl.when(i < ctx_blocks)
    def _():
        move_block(hc_ref)

    @pl.when(i >= ctx_blocks)
    def _():
        move_block(hs_ref)


def _dispatch(fill_tiles, pos, h2_c, h2_s, n_rows):
    d = h2_c.shape[1]
    ctx_blocks = h2_c.shape[0] // ROW_BLOCK
    n_blk = pos.shape[0] // ROW_BLOCK
    return pl.pallas_call(
        functools.partial(_dispatch_kernel, ctx_blocks=ctx_blocks),
        out_shape=jax.ShapeDtypeStruct((n_rows, d), F32),
        grid_spec=pltpu.PrefetchScalarGridSpec(
            num_scalar_prefetch=1, grid=(n_blk,),
            in_specs=[pl.BlockSpec((1, 1, ROW_BLOCK * TOP_K), lambda i, ft: (i, 0, 0),
                                   memory_space=pltpu.SMEM),
                      pl.BlockSpec((ROW_BLOCK, d), lambda i, ft: (jnp.minimum(i, ctx_blocks - 1), 0)),
                      pl.BlockSpec((ROW_BLOCK, d), lambda i, ft: (jnp.maximum(i - ctx_blocks, 0), 0))],
            out_specs=pl.BlockSpec(memory_space=pl.ANY),
            scratch_shapes=[pltpu.VMEM((EXPERT_TILE, d), F32), pltpu.SemaphoreType.DMA(()),
                            pltpu.SemaphoreType.DMA(())]),
        compiler_params=_params("arbitrary"),
        name="expert_dispatch",
    )(fill_tiles, pos.reshape(n_blk, 1, ROW_BLOCK * TOP_K), h2_c, h2_s)


def _moe_kernel(te_ref, nt_ref, x_ref, wi_ref, bi_ref, wo_ref, bo_ref, y_ref):
    i = pl.program_id(0)

    @pl.when(i < nt_ref[0])
    def _():
        gu = jnp.dot(x_ref[...].astype(BF16), wi_ref[0].astype(BF16),
                     preferred_element_type=F32) + bi_ref[0]
        g_lin = jnp.minimum(gu[:, :D_EXPERT], SWIGLU_LIMIT)
        u_lin = jnp.clip(gu[:, D_EXPERT:], -SWIGLU_LIMIT, SWIGLU_LIMIT)
        hid = (u_lin + 1.0) * (g_lin * jax.nn.sigmoid(SWIGLU_ALPHA * g_lin))
        y_ref[...] = jnp.dot(hid.astype(BF16), wo_ref[0].astype(BF16),
                             preferred_element_type=F32) + bo_ref[0]

    @pl.when(i >= nt_ref[0])
    def _():
        y_ref[...] = jnp.zeros_like(y_ref)


def _grouped_mlp(tile_expert, n_tiles, xs, w_in, b_in, w_out, b_out):
    p, d = xs.shape
    max_tiles = p // EXPERT_TILE

    def active(i, te, nt):
        return jnp.minimum(i, nt[0] - 1)

    def expert(i, te, nt):
        return te[active(i, te, nt)]

    grid_spec = pltpu.PrefetchScalarGridSpec(
        num_scalar_prefetch=2,
        grid=(max_tiles,),
        in_specs=[pl.BlockSpec((EXPERT_TILE, d), lambda i, te, nt: (active(i, te, nt), 0)),
                  pl.BlockSpec((1, d, 2 * D_EXPERT), lambda i, te, nt: (expert(i, te, nt), 0, 0)),
                  pl.BlockSpec((1, 1, 2 * D_EXPERT), lambda i, te, nt: (expert(i, te, nt), 0, 0)),
                  pl.BlockSpec((1, D_EXPERT, d), lambda i, te, nt: (expert(i, te, nt), 0, 0)),
                  pl.BlockSpec((1, 1, d), lambda i, te, nt: (expert(i, te, nt), 0, 0))],
        out_specs=pl.BlockSpec((EXPERT_TILE, d), lambda i, te, nt: (i, 0)))
    return pl.pallas_call(
        _moe_kernel,
        out_shape=jax.ShapeDtypeStruct((p, d), F32),
        grid_spec=grid_spec,
        compiler_params=_params("arbitrary"),
        name="grouped_expert_mlp",
    )(tile_expert, n_tiles, xs, w_in, b_in, w_out, b_out)


def _combine_kernel(pos_ref, next_pos_ref, wt_ref, x1_ref, mod_ref, g_ref, ys_ref, o_ref,
                    buf_ref, sems):
    d = D_MODEL
    i = pl.program_id(0)
    slot = i % 2

    def row_copy(src_row, dst_slot, k, r):
        return pltpu.make_async_copy(ys_ref.at[pl.ds(src_row, 1)],
                                     buf_ref.at[dst_slot, k, pl.ds(r, 1)], sems.at[dst_slot])

    @pl.when(i == 0)
    def _():
        _start_row_copies(lambda r, k: row_copy(pos_ref[0, 0, r * TOP_K + k], slot, k, r))

    @pl.when(i + 1 < pl.num_programs(0))
    def _():
        _start_row_copies(lambda r, k: row_copy(next_pos_ref[0, 0, r * TOP_K + k], 1 - slot, k, r))

    _wait_row_copies(row_copy(0, slot, 0, 0))
    wt = wt_ref[...]
    moe = wt[:, 0:1] * buf_ref[slot, 0]
    for k in range(1, TOP_K):
        moe = moe + wt[:, k:k + 1] * buf_ref[slot, k]
    mod = mod_ref[0]
    o_ref[...] = x1_ref[...] + mod[:, 5 * d:6 * d] * _rms(moe, g_ref[3:4, :])


def _combine(pos, wt, x1, mod, mod_row, gains, ys):
    t, d = x1.shape
    n_blk = t // ROW_BLOCK
    pos = pos.reshape(n_blk, 1, ROW_BLOCK * TOP_K)
    pos_block = lambda index: pl.BlockSpec((1, 1, ROW_BLOCK * TOP_K), lambda i: (index(i), 0, 0),
                                           memory_space=pltpu.SMEM)
    return pl.pallas_call(
        _combine_kernel,
        out_shape=jax.ShapeDtypeStruct((t, d), F32),
        grid=(n_blk,),
        in_specs=[pos_block(lambda i: i),
                  pos_block(lambda i: jnp.minimum(i + 1, n_blk - 1)),
                  pl.BlockSpec((ROW_BLOCK, TOP_K), lambda i: (i, 0)),
                  pl.BlockSpec((ROW_BLOCK, d), lambda i: (i, 0)),
                  pl.BlockSpec((1, 1, mod.shape[-1]), lambda i: (mod_row(i), 0, 0)),
                  pl.BlockSpec(gains.shape, lambda i: (0, 0)),
                  pl.BlockSpec(memory_space=pl.ANY)],
        out_specs=pl.BlockSpec((ROW_BLOCK, d), lambda i: (i, 0)),
        scratch_shapes=[pltpu.VMEM((2, TOP_K, ROW_BLOCK, d), F32), pltpu.SemaphoreType.DMA((2,))],
        compiler_params=_params("arbitrary"),
        name=f"expert_combine_{n_blk}",
    )(pos, pos, wt, x1, mod, gains, ys)


def _routing_tables(idx):
    t = idx.shape[0]
    onehot = idx[:, :, None] == jnp.arange(N_EXPERTS, dtype=jnp.int32)
    mask = jnp.any(onehot, axis=1).astype(jnp.int32)
    csum = jnp.cumsum(mask, axis=0)
    rank = csum - mask
    counts = csum[-1]
    tiles = (counts + EXPERT_TILE - 1) // EXPERT_TILE
    tile_end = jnp.cumsum(tiles)
    row_start = (tile_end - tiles) * EXPERT_TILE
    dest = row_start[None, :] + rank
    pos = jnp.sum(jnp.where(onehot, dest[:, None, :], 0), axis=-1)
    max_tiles = (t * TOP_K) // EXPERT_TILE + N_EXPERTS
    tile_expert = jnp.sum(jnp.arange(max_tiles, dtype=jnp.int32)[:, None] >= tile_end[None, :],
                          axis=-1)
    tile_expert = jnp.minimum(tile_expert, N_EXPERTS - 1).astype(jnp.int32)
    unused = tile_end[-1] + jnp.arange(N_EXPERTS, dtype=tile_end.dtype)
    fill_tile = jnp.concatenate([tile_end - 1, unused])
    fill_live = jnp.concatenate([tiles > 0, unused < max_tiles])
    fill_tiles = jnp.stack([jnp.clip(fill_tile, 0, max_tiles - 1), fill_live.astype(fill_tile.dtype)])
    return (pos.astype(jnp.int32), tile_expert, tile_end[-1:].astype(jnp.int32),
            fill_tiles.astype(jnp.int32), max_tiles)


def kernel(x_prompt, x_sample, c, cache_k, cache_v, c_ctx, w_ada, b_ada, norm_gains, w_in,
           w_pool_mix, pool_scale, lambda_params, subln_g, w_pool_br, w_attn_br, w_out,
           w_router, b_router, w_moe_in, b_moe_in, w_moe_out, b_moe_out):
    depth = w_ada.shape[0]
    assert depth == 1, "one trunk layer"
    d = D_MODEL
    nb, ns, _ = x_prompt.shape
    db, dn, _ = x_sample.shape
    past = cache_k.shape[2]
    l = 0

    ctx_row = db
    mod_rows = -(-(db + 1) // 8) * 8
    cc = jnp.zeros((mod_rows, d), F32).at[:db].set(c).at[ctx_row].set(c_ctx)
    mod = _modulation(cc, w_ada[l], b_ada[l][None, :])[:, None, :]

    gains = norm_gains[l]
    w_in_b = w_in[l].astype(BF16)
    w_mix_b = w_pool_mix[l].astype(BF16)
    w_pb_b = w_pool_br[l].astype(BF16)
    w_ab_b = w_attn_br[l].astype(BF16)
    w_o_b = w_out[l].astype(BF16)
    ps = pool_scale[l][None, :]
    sg = subln_g[l][None, :]
    lp = lambda_params[l]
    wr = w_router[l]
    br = b_router[l][None, :]
    ck = cache_k[:, l].reshape(db, past, N_HEADS * QK_DIM).astype(BF16)
    cv = cache_v[:, l].reshape(db, past, N_HEADS * V_DIM).astype(BF16).transpose(0, 2, 1)

    ctx_mod = lambda i: ctx_row
    lat_mod = lambda i: i

    def mixer(x, mod_row, rope_tables, cache, tm, tq, keep_kv):
        outs = _in_projection(x, mod, mod_row, gains, w_in_b, rope_tables, tm, keep_kv)
        u, qt, k, vt, gp, ga = outs[:6]
        attn = _diff_attention(lp, sg, qt, k, vt, cache[0], cache[1], tq)
        x1, h2, idx, wt = _mixer_tail(x, u, attn, gp, ga, mod, mod_row, gains, w_mix_b, ps,
                                      w_pb_b, w_ab_b, w_o_b, wr, br, min(tm, TAIL_BLOCK))
        return x1, h2, idx, wt, outs[6:]

    x1_c, h2_c, idx_c, wt_c, (new_k, new_v) = mixer(x_prompt, ctx_mod, None, (None, None),
                                                    min(ns, 256), min(ns, 256), True)
    x1_s, h2_s, idx_s, wt_s, _ = mixer(x_sample, lat_mod, _rope_tables(dn), (ck, cv),
                                       512, 256, False)

    t_c = nb * ns
    t_s = db * dn
    idx_all = jnp.concatenate([idx_c.reshape(t_c, TOP_K), idx_s.reshape(t_s, TOP_K)], axis=0)
    pos, tile_expert, n_tiles, fill_tiles, max_tiles = _routing_tables(idx_all)

    n_rows = max_tiles * EXPERT_TILE
    xs = _dispatch(fill_tiles, pos, h2_c.reshape(t_c, d), h2_s.reshape(t_s, d), n_rows)
    ys = _grouped_mlp(tile_expert, n_tiles, xs, w_moe_in[l], b_moe_in[l][:, None, :],
                      w_moe_out[l], b_moe_out[l][:, None, :])

    y_c = _combine(pos[:t_c], wt_c.reshape(t_c, TOP_K), x1_c.reshape(t_c, d), mod,
                   ctx_mod, gains, ys)
    blocks_per_request = dn // ROW_BLOCK
    y_s = _combine(pos[t_c:], wt_s.reshape(t_s, TOP_K), x1_s.reshape(t_s, d), mod,
                   lambda i: i // blocks_per_request, gains, ys)

    return (y_c.reshape(nb, ns, d), y_s.reshape(db, dn, d),
            new_k.reshape(nb, 1, ns, N_HEADS, QK_DIM), new_v.reshape(nb, 1, ns, N_HEADS, V_DIM))
```
